```python
import math
import jax, jax.numpy as jnp
from jax import lax
import numpy as np

D_MODEL = 2048
BATCH = 16
SEQ = 2048
DEPTH = 1
DEC_BATCH = 1
DEC_SEQ = 16384
PAST_LEN = 128

HEAD_DIM = 128
DIL_WINDOWS = (128, 512, 2048)
DIL_RATES = (1, 4, 16)
N_DIL_GROUPS = 3
A_HEADS = 4
A_BLOCK = 64
B_Q_HEADS = 8
B_KV_HEADS = 2
B_GROUP = B_Q_HEADS // B_KV_HEADS
B_RADIUS = 128
B_BLOCK = 128
N_REL_HEADS = N_DIL_GROUPS * A_HEADS + B_Q_HEADS
REL_BUCKETS = 32
REL_MAX_DIST = 1024
N_EXPERTS = 16
EC_CAPACITY = 2
EXPERT_FF = 5632
NORM_EPS = 1e-6
NEG_INF = -1e30

A_QKV = N_DIL_GROUPS * A_HEADS * HEAD_DIM
A_OUT = A_HEADS * HEAD_DIM
B_Q = B_Q_HEADS * HEAD_DIM
B_KV = B_KV_HEADS * HEAD_DIM
IN_COLS = 3 * A_QKV + B_Q + 2 * B_KV + 2 * D_MODEL

kernel_name = "hybrid_dilated_window_ec_encoder"


def rmsnorm(x, g):
    xf = x.astype(jnp.float32)
    y = xf * lax.rsqrt(jnp.mean(xf * xf, axis=-1, keepdims=True) + NORM_EPS) * g.astype(jnp.float32)
    return y.astype(x.dtype)


def t5_bucket(rel):
    half = REL_BUCKETS // 2
    max_exact = half // 2
    n = jnp.abs(rel)
    base = jnp.where(rel > 0, half, 0)
    nf = jnp.maximum(n, 1).astype(jnp.float32)
    large = max_exact + (jnp.log(nf / max_exact) / math.log(REL_MAX_DIST / max_exact)
                         * (half - max_exact)).astype(jnp.int32)
    large = jnp.minimum(large, half - 1)
    return base + jnp.where(n < max_exact, n, large)


def banded_attention(q, k, v, table, dil, radius, block, sink=None):
    N, L, Hk, G, Dh = q.shape
    nb = -(-L // block)
    pad = nb * block - L
    qb = jnp.pad(q, ((0, 0), (0, pad), (0, 0), (0, 0), (0, 0))).reshape(N, nb, block, Hk, G, Dh)

    def windows(t):
        tp = jnp.pad(t, ((0, 0), (block, block + pad), (0, 0), (0, 0))).reshape(N, nb + 2, block, Hk, Dh)
        return jnp.concatenate([tp[:, :-2], tp[:, 1:-1], tp[:, 2:]], axis=2)

    kw, vw = windows(k), windows(v)
    t_q = jnp.arange(block)
    t_k = jnp.arange(3 * block) - block
    rel = t_k[None, :] - t_q[:, None]
    bias = table[t5_bucket(rel * dil)].astype(jnp.float32)
    bias = bias.transpose(2, 0, 1).reshape(Hk, G, block, 3 * block)
    kpos = jnp.arange(nb)[:, None] * block + t_k[None, :]
    valid = (jnp.abs(rel) <= radius)[None] & ((kpos >= 0) & (kpos < L))[:, None, :]
    s = jnp.einsum('nbqhgd,nbkhd->nbhgqk', qb, kw, preferred_element_type=jnp.float32) + bias
    s = jnp.where(valid[None, :, None, None], s, NEG_INF)
    m = jnp.max(s, axis=-1, keepdims=True)
    if sink is not None:
        sk = sink.astype(jnp.float32).reshape(1, 1, Hk, G, 1, 1)
        m = jnp.maximum(m, sk)
    p = jnp.exp(s - m)
    l = jnp.sum(p, axis=-1, keepdims=True)
    if sink is not None:
        l = l + jnp.exp(sk - m)
    o = jnp.einsum('nbhgqk,nbkhd->nbhgqd', p, vw.astype(jnp.float32)) / l
    lse = (m + jnp.log(l))[..., 0]
    o = o.transpose(0, 1, 4, 2, 3, 5).reshape(N, nb * block, Hk, G, Dh)[:, :L].astype(q.dtype)
    lse = lse.transpose(0, 1, 4, 2, 3).reshape(N, nb * block, Hk, G)[:, :L]
    return o, lse


def dilated_mixer(qa, ka, va, rel_table):
    Bn, S = qa.shape[0], qa.shape[1]
    outs, lses = [], []
    for g in range(N_DIL_GROUPS):
        d = DIL_RATES[g]
        radius = (DIL_WINDOWS[g] // 2) // d
        Ls = S // d

        def strided(t):
            return t.reshape(Bn, Ls, d, A_HEADS, HEAD_DIM).transpose(0, 2, 1, 3, 4).reshape(Bn * d, Ls, A_HEADS, HEAD_DIM)

        qg = strided(qa[:, :, g])[:, :, :, None, :]
        o, lse = banded_attention(qg, strided(ka[:, :, g]), strided(va[:, :, g]),
                                  rel_table[:, g * A_HEADS:(g + 1) * A_HEADS], d, radius, A_BLOCK)
        o = o[:, :, :, 0].reshape(Bn, d, Ls, A_HEADS, HEAD_DIM).transpose(0, 2, 1, 3, 4).reshape(Bn, S, A_HEADS, HEAD_DIM)
        lse = lse[..., 0].reshape(Bn, d, Ls, A_HEADS).transpose(0, 2, 1, 3).reshape(Bn, S, A_HEADS)
        outs.append(o)
        lses.append(lse)
    w = jax.nn.softmax(jnp.stack(lses, axis=0), axis=0)
    o = jnp.sum(w[..., None] * jnp.stack(outs, axis=0).astype(jnp.float32), axis=0)
    return o.astype(qa.dtype).reshape(Bn, S, A_OUT)


def window_gqa_mixer(qb, kb, vb, rel_table, sink):
    Bn, S = qb.shape[0], qb.shape[1]
    o, _ = banded_attention(qb, kb, vb, rel_table[:, N_DIL_GROUPS * A_HEADS:], 1, B_RADIUS, B_BLOCK, sink)
    return o.reshape(Bn, S, B_Q)


def expert_choice_ffn(h, w_router, w_gate, w_up, w_down):
    Bn, S, D = h.shape
    tok = h.reshape(Bn * S, D)
    n_tok = Bn * S
    cap = max(1, EC_CAPACITY * n_tok // N_EXPERTS)
    aff = jax.nn.softmax(jnp.dot(tok, w_router).astype(jnp.float32), axis=-1)
    gates, idx = lax.top_k(aff.T, cap)
    xe = tok[idx]
    hid = jax.nn.silu(jnp.einsum('ecd,edf->ecf', xe, w_gate)) * jnp.einsum('ecd,edf->ecf', xe, w_up)
    ye = jnp.einsum('ecf,efd->ecd', hid, w_down) * gates[..., None].astype(h.dtype)
    out = jnp.zeros_like(tok).at[idx.reshape(-1)].add(ye.reshape(-1, D).astype(tok.dtype))
    return out.reshape(Bn, S, D)


def encoder_layer(x, rel_table, norm_mix_g, w_in, q_norm_a, k_norm_a, q_norm_b, k_norm_b, sink_b,
                  w_proj_a, w_proj_b, w_out, norm_ffn_g, w_router, w_gate_e, w_up_e, w_down_e):
    Bn, S, _ = x.shape
    scale = HEAD_DIM ** -0.5
    h = rmsnorm(x, norm_mix_g)
    z = jnp.dot(h, w_in)
    cuts = [int(c) for c in np.cumsum([A_QKV, A_QKV, A_QKV, B_Q, B_KV, B_KV, D_MODEL])]
    qa, ka, va, qb, kb, vb, ga, gb = jnp.split(z, cuts, axis=-1)
    ash = (Bn, S, N_DIL_GROUPS, A_HEADS, HEAD_DIM)
    qa = rmsnorm(qa.reshape(ash), q_norm_a) * scale
    ka = rmsnorm(ka.reshape(ash), k_norm_a)
    o_a = dilated_mixer(qa, ka, va.reshape(ash), rel_table)
    qb = rmsnorm(qb.reshape(Bn, S, B_KV_HEADS, B_GROUP, HEAD_DIM), q_norm_b) * scale
    kb = rmsnorm(kb.reshape(Bn, S, B_KV_HEADS, HEAD_DIM), k_norm_b)
    o_b = window_gqa_mixer(qb, kb, vb.reshape(Bn, S, B_KV_HEADS, HEAD_DIM), rel_table, sink_b)
    merged = jax.nn.sigmoid(ga) * jnp.dot(o_a, w_proj_a) + jax.nn.sigmoid(gb) * jnp.dot(o_b, w_proj_b)
    x = x + jnp.dot(merged, w_out)
    x = x + expert_choice_ffn(rmsnorm(x, norm_ffn_g), w_router, w_gate_e, w_up_e, w_down_e)
    return x


def setup_inputs(seed: int = 0) -> dict:
    key = jax.random.key(seed)
    ks = jax.random.split(key, 20)
    f32 = jnp.float32
    nrm = lambda k, shape, s: jax.random.normal(k, shape, f32) * s
    return {
        "x_prompt": nrm(ks[0], (BATCH, SEQ, D_MODEL), 1.0),
        "x_sample": nrm(ks[1], (DEC_BATCH, DEC_SEQ, D_MODEL), 1.0),
        "rel_table": nrm(ks[2], (REL_BUCKETS, N_REL_HEADS), 0.2),
        "norm_mix_g": 1.0 + nrm(ks[3], (DEPTH, D_MODEL), 0.02),
        "w_in": nrm(ks[4], (DEPTH, D_MODEL, IN_COLS), D_MODEL ** -0.5),
        "q_norm_a": 1.0 + nrm(ks[5], (DEPTH, HEAD_DIM), 0.02),
        "k_norm_a": 1.0 + nrm(ks[6], (DEPTH, HEAD_DIM), 0.02),
        "q_norm_b": 1.0 + nrm(ks[7], (DEPTH, HEAD_DIM), 0.02),
        "k_norm_b": 1.0 + nrm(ks[8], (DEPTH, HEAD_DIM), 0.02),
        "sink_b": nrm(ks[9], (DEPTH, B_Q_HEADS), 0.5),
        "w_proj_a": nrm(ks[10], (DEPTH, A_OUT, D_MODEL), A_OUT ** -0.5),
        "w_proj_b": nrm(ks[11], (DEPTH, B_Q, D_MODEL), B_Q ** -0.5),
        "w_out": nrm(ks[12], (DEPTH, D_MODEL, D_MODEL), D_MODEL ** -0.5),
        "norm_ffn_g": 1.0 + nrm(ks[13], (DEPTH, D_MODEL), 0.02),
        "w_router": nrm(ks[14], (DEPTH, D_MODEL, N_EXPERTS), D_MODEL ** -0.5),
        "w_gate_e": nrm(ks[15], (DEPTH, N_EXPERTS, D_MODEL, EXPERT_FF), D_MODEL ** -0.5),
        "w_up_e": nrm(ks[16], (DEPTH, N_EXPERTS, D_MODEL, EXPERT_FF), D_MODEL ** -0.5),
        "w_down_e": nrm(ks[17], (DEPTH, N_EXPERTS, EXPERT_FF, D_MODEL), EXPERT_FF ** -0.5),
    }


def reference(x_prompt, x_sample, rel_table, norm_mix_g, w_in, q_norm_a, k_norm_a, q_norm_b, k_norm_b,
              sink_b, w_proj_a, w_proj_b, w_out, norm_ffn_g, w_router, w_gate_e, w_up_e, w_down_e):
    y_prompt = x_prompt
    y_sample = x_sample
    for l in range(DEPTH):
        layer_params = (rel_table, norm_mix_g[l], w_in[l], q_norm_a[l], k_norm_a[l], q_norm_b[l],
                        k_norm_b[l], sink_b[l], w_proj_a[l], w_proj_b[l], w_out[l], norm_ffn_g[l],
                        w_router[l], w_gate_e[l], w_up_e[l], w_down_e[l])
        y_prompt = encoder_layer(y_prompt, *layer_params)
        y_sample = encoder_layer(y_sample, *layer_params)
    return (y_prompt, y_sample)
```

```python
import functools
import math

import jax
import jax.numpy as jnp
from jax import lax
from jax.experimental import pallas as pl
from jax.experimental.pallas import tpu as pltpu

HEAD_DIM = 128
DIL_WINDOWS = (128, 512, 2048)
DIL_RATES = (1, 4, 16)
N_DIL_GROUPS = 3
A_HEADS = 4
B_Q_HEADS = 8
B_KV_HEADS = 2
B_GROUP = B_Q_HEADS // B_KV_HEADS
B_RADIUS = 128
REL_BUCKETS = 32
REL_MAX_DIST = 1024
EC_CAPACITY = 2
NORM_EPS = 1e-6
NEG_INF = -1e30

A_QKV = N_DIL_GROUPS * A_HEADS * HEAD_DIM
A_OUT = A_HEADS * HEAD_DIM
B_Q = B_Q_HEADS * HEAD_DIM
B_KV = B_KV_HEADS * HEAD_DIM

LANES = 128
GROUP_COLS = A_HEADS * HEAD_DIM
VMEM_LIMIT = 60 * 1024 * 1024


def _cparams(sem, vmem=VMEM_LIMIT):
    return pltpu.CompilerParams(dimension_semantics=sem, vmem_limit_bytes=vmem)


def _inproj_kernel(x_ref, g_ref, w_ref, z_ref, h_scr):
    @pl.when(pl.program_id(1) == 0)
    def _():
        x = x_ref[...]
        ms = jnp.mean(x * x, axis=-1, keepdims=True)
        h_scr[...] = (x * lax.rsqrt(ms + NORM_EPS) * g_ref[...]).astype(jnp.bfloat16)

    z_ref[...] = jnp.dot(h_scr[...], w_ref[...],
                         preferred_element_type=jnp.float32).astype(jnp.bfloat16)


def _inproj(x, g, w_bf16, tm, tn):
    n, d = x.shape
    cols = w_bf16.shape[1]
    return pl.pallas_call(
        _inproj_kernel,
        grid=(n // tm, cols // tn),
        in_specs=[
            pl.BlockSpec((tm, d), lambda i, j: (i, 0)),
            pl.BlockSpec((1, d), lambda i, j: (0, 0)),
            pl.BlockSpec((d, tn), lambda i, j: (0, j)),
        ],
        out_specs=pl.BlockSpec((tm, tn), lambda i, j: (i, j)),
        out_shape=jax.ShapeDtypeStruct((n, cols), jnp.bfloat16),
        scratch_shapes=[pltpu.VMEM((tm, d), jnp.bfloat16)],
        compiler_params=_cparams(("parallel", "arbitrary")),
        name="inproj",
    )(x, g.reshape(1, d), w_bf16)


def _t5_bucket(rel):
    half = REL_BUCKETS // 2
    max_exact = half // 2
    n = jnp.abs(rel)
    base = jnp.where(rel > 0, half, 0)
    nf = jnp.maximum(n, 1).astype(jnp.float32)
    large = max_exact + (jnp.log(nf / max_exact) / math.log(REL_MAX_DIST / max_exact)
                         * (half - max_exact)).astype(jnp.int32)
    large = jnp.minimum(large, half - 1)
    return base + jnp.where(n < max_exact, n, large)


def _bias_tile(table_cols, dil, radius, tq):
    tk = tq + 2 * radius
    rel = (jnp.arange(tk)[None, :] - radius) - jnp.arange(tq)[:, None]
    bias = table_cols[_t5_bucket(rel * dil)].astype(jnp.float32)
    bias = jnp.where((jnp.abs(rel) <= radius)[..., None], bias, NEG_INF)
    return bias.transpose(2, 0, 1)


def _head_norm(x, w):
    xf = x.astype(jnp.float32)
    ms = jnp.mean(xf * xf, axis=-1, keepdims=True)
    return xf * lax.rsqrt(ms + NORM_EPS) * w


def _attn_kernel(*refs, tq, radius, shared_kv, with_sink, with_lse, np_rows, len_p, len_s):
    (q_ref, kp_ref, kc_ref, kn_ref, vp_ref, vc_ref, vn_ref, bias_ref, qw_ref, kw_ref) = refs[:10]
    pos = 10
    sink_ref = None
    if with_sink:
        sink_ref = refs[pos]
        pos += 1
    o_ref = refs[pos]
    lse_ref = refs[pos + 1] if with_lse else None

    tk = tq + 2 * radius
    q0 = pl.program_id(1) * tq
    lo_p = (q0 // len_p) * len_p
    lo_s = np_rows + ((q0 - np_rows) // len_s) * len_s
    in_p = q0 < np_rows
    lo = jnp.where(in_p, lo_p, lo_s)
    hi = lo + jnp.where(in_p, len_p, len_s)
    kpos = q0 - radius + lax.broadcasted_iota(jnp.int32, (1, tk), 1)
    valid = (kpos >= lo) & (kpos < hi)

    scale = HEAD_DIM ** -0.5
    k_all = jnp.concatenate([kp_ref[...], kc_ref[...], kn_ref[...]], axis=0)
    v_all = jnp.concatenate([vp_ref[...], vc_ref[...], vn_ref[...]], axis=0)
    qw = qw_ref[...]
    kw = kw_ref[...]
    if shared_kv:
        k_sh = _head_norm(k_all, kw).astype(jnp.bfloat16)

    lane = lax.broadcasted_iota(jnp.int32, (tq, LANES), 1)
    lse_tile = jnp.zeros((tq, LANES), jnp.float32)
    for h in range(A_HEADS):
        cs = slice(h * HEAD_DIM, (h + 1) * HEAD_DIM)
        qh = (_head_norm(q_ref[:, cs], qw) * scale).astype(jnp.bfloat16)
        if shared_kv:
            kh, vh = k_sh, v_all
        else:
            kh = _head_norm(k_all[:, cs], kw).astype(jnp.bfloat16)
            vh = v_all[:, cs]
        s = lax.dot_general(qh, kh, (((1,), (1,)), ((), ())),
                            preferred_element_type=jnp.float32)
        s = jnp.where(valid, s + bias_ref[h], NEG_INF)
        m = jnp.max(s, axis=-1, keepdims=True)
        if with_sink:
            m = jnp.maximum(m, sink_ref[h])
        p = jnp.exp(s - m)
        l = jnp.sum(p, axis=-1, keepdims=True)
        if with_sink:
            l = l + jnp.exp(sink_ref[h] - m)
        o = jnp.dot(p.astype(jnp.bfloat16), vh, preferred_element_type=jnp.float32) / l
        o_ref[:, cs] = o.astype(o_ref.dtype)
        if with_lse:
            lse_tile = jnp.where(lane == h, m + jnp.log(l), lse_tile)
    if with_lse:
        lse_ref[...] = lse_tile


def _banded_attention(z, *, dil, radius, tq, q_col, k_col, v_col, kv_width, n_par, bias,
                      q_w, k_w, sink, with_lse, np_tok, seq_p, seq_s, out_cols):
    ntok, in_cols = z.shape
    rows = ntok // dil
    zv = z.reshape(rows, dil * in_cols)
    nq = rows // tq
    hb = tq // radius
    n_halo = rows // radius
    shared_kv = kv_width == HEAD_DIM
    q_blk = in_cols // GROUP_COLS
    kv_blk = in_cols // kv_width
    if dil > 1:
        qmap = lambda r, i: (i, r * q_blk + q_col // GROUP_COLS)
        kc = lambda c: (lambda r, i: (i, r * kv_blk + c // kv_width))
        kp = lambda c: (lambda r, i: (jnp.maximum(i * hb - 1, 0), r * kv_blk + c // kv_width))
        kn = lambda c: (lambda r, i: (jnp.minimum((i + 1) * hb, n_halo - 1), r * kv_blk + c // kv_width))
        omap = lambda r, i: (i, r)
    else:
        qmap = lambda r, i: (i, q_col // GROUP_COLS + r)
        kc = lambda c: (lambda r, i: (i, c // kv_width + r))
        kp = lambda c: (lambda r, i: (jnp.maximum(i * hb - 1, 0), c // kv_width + r))
        kn = lambda c: (lambda r, i: (jnp.minimum((i + 1) * hb, n_halo - 1), c // kv_width + r))
        omap = lambda r, i: (i, r)

    tk = tq + 2 * radius
    in_specs = [
        pl.BlockSpec((tq, GROUP_COLS), qmap),
        pl.BlockSpec((radius, kv_width), kp(k_col)),
        pl.BlockSpec((tq, kv_width), kc(k_col)),
        pl.BlockSpec((radius, kv_width), kn(k_col)),
        pl.BlockSpec((radius, kv_width), kp(v_col)),
        pl.BlockSpec((tq, kv_width), kc(v_col)),
        pl.BlockSpec((radius, kv_width), kn(v_col)),
        pl.BlockSpec((A_HEADS, tq, tk), (lambda r, i: (r, 0, 0)) if (dil == 1 and n_par > 1)
                     else (lambda r, i: (0, 0, 0))),
        pl.BlockSpec((1, HEAD_DIM), lambda r, i: (0, 0)),
        pl.BlockSpec((1, HEAD_DIM), lambda r, i: (0, 0)),
    ]
    args = [zv] * 7 + [bias, q_w.reshape(1, HEAD_DIM), k_w.reshape(1, HEAD_DIM)]
    if sink is not None:
        in_specs.append(pl.BlockSpec((A_HEADS, 1, 1), lambda r, i: (r, 0, 0)))
        args.append(sink.reshape(-1, 1, 1).astype(jnp.float32))
    out_specs = [pl.BlockSpec((tq, GROUP_COLS), omap)]
    out_shape = [jax.ShapeDtypeStruct((rows, n_par * GROUP_COLS), jnp.bfloat16)]
    if with_lse:
        out_specs.append(pl.BlockSpec((tq, LANES), omap))
        out_shape.append(jax.ShapeDtypeStruct((rows, n_par * LANES), jnp.float32))
    kern = functools.partial(
        _attn_kernel, tq=tq, radius=radius, shared_kv=shared_kv, with_sink=sink is not None,
        with_lse=with_lse, np_rows=np_tok // dil, len_p=seq_p // dil, len_s=seq_s // dil)
    outs = pl.pallas_call(
        kern, grid=(n_par, nq), in_specs=in_specs, out_specs=out_specs, out_shape=out_shape,
        compiler_params=_cparams(("parallel", "arbitrary")),
        name=f"band_attn_d{dil}_r{radius}",
    )(*args)
    o = outs[0].reshape(ntok, out_cols)
    if with_lse:
        return o, outs[1].reshape(ntok, LANES)
    return o


def _merge_kernel(x_ref, oa0_ref, oa1_ref, oa2_ref, l0_ref, l1_ref, l2_ref, ob_ref, ga_ref, gb_ref,
                  wpa_ref, wpb_ref, wo_ref, g2_ref, wr_ref, x1_ref, h2_ref, aff_ref, *, tm):
    l0, l1, l2 = l0_ref[...], l1_ref[...], l2_ref[...]
    mx = jnp.maximum(jnp.maximum(l0, l1), l2)
    e0, e1, e2 = jnp.exp(l0 - mx), jnp.exp(l1 - mx), jnp.exp(l2 - mx)
    den = e0 + e1 + e2
    w0, w1, w2 = e0 / den, e1 / den, e2 / den
    parts = []
    for h in range(A_HEADS):
        cs = slice(h * HEAD_DIM, (h + 1) * HEAD_DIM)
        parts.append(w0[:, h:h + 1] * oa0_ref[:, cs].astype(jnp.float32)
                     + w1[:, h:h + 1] * oa1_ref[:, cs].astype(jnp.float32)
                     + w2[:, h:h + 1] * oa2_ref[:, cs].astype(jnp.float32))
    o_a = jnp.concatenate(parts, axis=1).astype(jnp.bfloat16)
    pa = jnp.dot(o_a, wpa_ref[...], preferred_element_type=jnp.float32)
    pb = jnp.dot(ob_ref[...], wpb_ref[...], preferred_element_type=jnp.float32)
    merged = (jax.nn.sigmoid(ga_ref[...].astype(jnp.float32)) * pa
              + jax.nn.sigmoid(gb_ref[...].astype(jnp.float32)) * pb)
    x1 = x_ref[...] + jnp.dot(merged.astype(jnp.bfloat16), wo_ref[...],
                              preferred_element_type=jnp.float32)
    x1_ref[...] = x1
    ms = jnp.mean(x1 * x1, axis=-1, keepdims=True)
    h2 = x1 * lax.rsqrt(ms + NORM_EPS) * g2_ref[...]
    h2_ref[...] = h2
    logits = lax.dot_general(wr_ref[...], h2.astype(jnp.bfloat16), (((1,), (1,)), ((), ())),
                             preferred_element_type=jnp.float32)
    mx2 = jnp.max(logits, axis=0, keepdims=True)
    ex = jnp.exp(logits - mx2)
    aff = ex / jnp.sum(ex, axis=0, keepdims=True)
    for j in range(tm // LANES):
        aff_ref[j] = aff[:, j * LANES:(j + 1) * LANES]


def _merge(x, oas, lses, ob, z, wpa, wpb, wo, g2, wr_t, tm):
    n, d = x.shape
    n_exp = wr_t.shape[0]
    in_cols = z.shape[1]
    ga_blk = (3 * A_QKV + B_Q + 2 * B_KV) // d
    row = lambda i: (i, 0)
    const = lambda i: (0, 0)
    in_specs = [
        pl.BlockSpec((tm, d), row),
        pl.BlockSpec((tm, A_OUT), row), pl.BlockSpec((tm, A_OUT), row), pl.BlockSpec((tm, A_OUT), row),
        pl.BlockSpec((tm, LANES), row), pl.BlockSpec((tm, LANES), row), pl.BlockSpec((tm, LANES), row),
        pl.BlockSpec((tm, B_Q), row),
        pl.BlockSpec((tm, d), lambda i: (i, ga_blk)),
        pl.BlockSpec((tm, d), lambda i: (i, ga_blk + 1)),
        pl.BlockSpec((A_OUT, d), const),
        pl.BlockSpec((B_Q, d), const),
        pl.BlockSpec((d, d), const),
        pl.BlockSpec((1, d), const),
        pl.BlockSpec((n_exp, d), const),
    ]
    out_specs = [
        pl.BlockSpec((tm, d), row),
        pl.BlockSpec((tm, d), row),
        pl.BlockSpec((tm // LANES, n_exp, LANES), lambda i: (i, 0, 0)),
    ]
    out_shape = [
        jax.ShapeDtypeStruct((n, d), jnp.float32),
        jax.ShapeDtypeStruct((n, d), jnp.float32),
        jax.ShapeDtypeStruct((n // LANES, n_exp, LANES), jnp.float32),
    ]
    return pl.pallas_call(
        functools.partial(_merge_kernel, tm=tm),
        grid=(n // tm,), in_specs=in_specs, out_specs=out_specs, out_shape=out_shape,
        compiler_params=_cparams(("parallel",)),
        name="merge_proj_router",
    )(x, oas[0], oas[1], oas[2], lses[0], lses[1], lses[2], ob, z, z, wpa, wpb, wo,
      g2.reshape(1, d), wr_t)


def _prefix_counts(flag_f32, tri, tot_scr, off_scr, nc):
    n_exp = flag_f32.shape[1]
    incl = jnp.dot(flag_f32.astype(jnp.bfloat16).reshape(nc * n_exp, LANES), tri,
                   preferred_element_type=jnp.float32).reshape(nc, n_exp, LANES)
    tot_scr[...] = jnp.broadcast_to(incl[:, :, LANES - 1:LANES], (nc, n_exp, LANES))

    def body(c, run):
        off_scr[c] = run
        return run + tot_scr[c]

    lax.fori_loop(0, nc, body, jnp.zeros((n_exp, LANES), jnp.float32))
    return off_scr[...] + incl - flag_f32


def _route_kernel(aff_ref, pos_ref, off_ref, tot_scr, off_scr, *, cap, nc):
    n_exp = aff_ref.shape[1]
    capf = jnp.float32(cap)

    def count(mask):
        c = jnp.sum(jnp.where(mask, 1.0, 0.0), axis=0)
        return jnp.sum(c, axis=-1, keepdims=True)

    def bit_body(k, t):
        cand = t | jnp.left_shift(jnp.int32(1), 30 - k)
        bits = pltpu.bitcast(aff_ref[...], jnp.int32)
        return jnp.where(count(bits >= cand[None]) >= capf, cand, t)

    t = lax.fori_loop(0, 31, bit_body, jnp.zeros((n_exp, 1), jnp.int32))
    bits = pltpu.bitcast(aff_ref[...], jnp.int32)
    gt = bits > t[None]
    eq = bits == t[None]
    need = capf - count(gt)
    rows = lax.broadcasted_iota(jnp.int32, (LANES, LANES), 0)
    cols = lax.broadcasted_iota(jnp.int32, (LANES, LANES), 1)
    tri = jnp.where(rows <= cols, 1.0, 0.0).astype(jnp.bfloat16)
    eq_f = jnp.where(eq, 1.0, 0.0)
    tie_rank = _prefix_counts(eq_f, tri, tot_scr, off_scr, nc)
    sel = gt | (eq & (tie_rank < need[None]))
    sel_f = jnp.where(sel, 1.0, 0.0)
    slot = _prefix_counts(sel_f, tri, tot_scr, off_scr, nc)
    pos_ref[...] = jnp.where(sel, slot, -1.0).astype(jnp.int32)
    off_ref[...] = off_scr[...].astype(jnp.int32)


def _route(aff, cap):
    nc, n_exp, _ = aff.shape
    full = pl.BlockSpec((nc, n_exp, LANES), lambda: (0, 0, 0))
    return pl.pallas_call(
        functools.partial(_route_kernel, cap=cap, nc=nc),
        in_specs=[full], out_specs=[full, full],
        out_shape=[jax.ShapeDtypeStruct((nc, n_exp, LANES), jnp.int32)] * 2,
        scratch_shapes=[pltpu.VMEM((nc, n_exp, LANES), jnp.float32)] * 2,
        compiler_params=pltpu.CompilerParams(vmem_limit_bytes=VMEM_LIMIT),
        name="route_select",
    )(aff)


def _split3(x):
    a = x.astype(jnp.bfloat16).astype(jnp.float32)
    r = x - a
    b = r.astype(jnp.bfloat16).astype(jnp.float32)
    c = r - b
    return a, b, c


def _compact_kernel(off_smem, pos_ref, aff_ref, idx_ref, gate_ref, acc_scr, *, nc, n_blk, tok_base):
    e = pl.program_id(0)
    stride = nc + 1
    sub = lax.broadcasted_iota(jnp.int32, (LANES, LANES), 0)
    sub16 = lax.broadcasted_iota(jnp.int32, (16, LANES), 0)
    lane = lax.broadcasted_iota(jnp.int32, (1, LANES), 1)

    def block_body(sb, c_first):
        lo_slot = sb * LANES

        def skip_cond(c):
            return jnp.logical_and(c < nc, off_smem[e * stride + jnp.minimum(c + 1, nc)] <= lo_slot)

        c_first = lax.while_loop(skip_cond, lambda c: c + 1, c_first)
        acc_scr[...] = jnp.zeros_like(acc_scr)

        def take_cond(c):
            return jnp.logical_and(c < nc, off_smem[e * stride + jnp.minimum(c, nc)] < lo_slot + LANES)

        def take(c):
            rel = pos_ref[c, pl.ds(e, 1), :] - lo_slot
            onehot = jnp.where(sub == rel, 1.0, 0.0).astype(jnp.bfloat16)
            tok = tok_base + c * LANES + lane
            g1, g2, g3 = _split3(aff_ref[c, pl.ds(e, 1), :])
            lhs = jnp.where(sub16 == 0, (tok >> 8).astype(jnp.float32),
                  jnp.where(sub16 == 1, (tok & 255).astype(jnp.float32),
                  jnp.where(sub16 == 2, g1,
                  jnp.where(sub16 == 3, g2,
                  jnp.where(sub16 == 4, g3, 0.0))))).astype(jnp.bfloat16)
            acc_scr[...] += lax.dot_general(lhs, onehot, (((1,), (1,)), ((), ())),
                                            preferred_element_type=jnp.float32)
            return c + 1

        lax.while_loop(take_cond, take, c_first)
        acc = acc_scr[...]
        idx_ref[0, pl.ds(sb, 1), :] = (acc[0:1] * 256.0 + acc[1:2]).astype(jnp.int32)
        gate_ref[0, pl.ds(sb, 1), :] = (acc[2:3] + acc[3:4]) + acc[4:5]
        return c_first

    lax.fori_loop(0, n_blk, block_body, jnp.int32(0))


def _compact(offs_flat, pos, aff, cap, tok_base):
    nc, n_exp, _ = pos.shape
    n_blk = cap // LANES
    full = pl.BlockSpec((nc, n_exp, LANES), lambda e, off: (0, 0, 0))
    out = pl.BlockSpec((1, n_blk, LANES), lambda e, off: (e, 0, 0))
    return pl.pallas_call(
        functools.partial(_compact_kernel, nc=nc, n_blk=n_blk, tok_base=tok_base),
        grid_spec=pltpu.PrefetchScalarGridSpec(
            num_scalar_prefetch=1, grid=(n_exp,), in_specs=[full, full], out_specs=[out, out],
            scratch_shapes=[pltpu.VMEM((16, LANES), jnp.float32)]),
        out_shape=[jax.ShapeDtypeStruct((n_exp, n_blk, LANES), jnp.int32),
                   jax.ShapeDtypeStruct((n_exp, n_blk, LANES), jnp.float32)],
        compiler_params=_cparams(("arbitrary",)),
        name="route_compact",
    )(offs_flat, pos, aff)


def _route_set(aff, cap, tok_base):
    nc, n_exp, _ = aff.shape
    pos, off = _route(aff, cap)
    offs = jnp.concatenate([off[:, :, 0].T, jnp.full((n_exp, 1), cap, jnp.int32)], axis=1)
    idx, gate = _compact(offs.reshape(-1), pos, aff, cap, tok_base)
    return idx.reshape(n_exp, cap), gate.reshape(n_exp, cap)


def _moe_kernel(idx_cur, idx_nxt, gate_ref, wg_ref, wu_ref, wd_ref, h2_hbm, x1_hbm, out_hbm,
                xe32, xe16, acc, orow, sems, *, tm, n_tiles, n_f):
    del x1_hbm
    e, t, f = pl.program_id(0), pl.program_id(1), pl.program_id(2)
    n = e * pl.num_programs(1) + t

    def gather_rows(src_hbm, idx_smem, dst, sem):
        def body(s, _):
            tok = idx_smem[0, 0, s]
            pltpu.make_async_copy(src_hbm.at[pl.ds(tok, 1)], dst.at[pl.ds(s, 1)], sem).start()
            return 0
        lax.fori_loop(0, tm, body, 0)

    def wait_rows(src_hbm, dst, sem):
        pltpu.make_async_copy(src_hbm.at[pl.ds(0, tm)], dst, sem).wait()

    @pl.when(f == 0)
    def _():
        @pl.when(n == 0)
        def _():
            gather_rows(h2_hbm, idx_cur, xe32, sems.at[0])

        wait_rows(h2_hbm, xe32, sems.at[0])
        xe16[...] = xe32[...].astype(jnp.bfloat16)

        @pl.when(n + 1 < n_tiles)
        def _():
            gather_rows(h2_hbm, idx_nxt, xe32, sems.at[0])

        @pl.when(n > 0)
        def _():
            pltpu.make_async_copy(orow, out_hbm.at[pl.ds(0, tm)], sems.at[2]).wait()

        gather_rows(out_hbm, idx_cur, orow, sems.at[1])

    x = xe16[...]
    gp = jnp.dot(x, wg_ref[0].astype(jnp.bfloat16), preferred_element_type=jnp.float32)
    up = jnp.dot(x, wu_ref[0].astype(jnp.bfloat16), preferred_element_type=jnp.float32)
    hid = (gp * jax.nn.sigmoid(gp) * up).astype(jnp.bfloat16)
    part = jnp.dot(hid, wd_ref[0].astype(jnp.bfloat16), preferred_element_type=jnp.float32)

    @pl.when(f == 0)
    def _():
        acc[...] = part

    @pl.when(f > 0)
    def _():
        acc[...] += part

    @pl.when(f == n_f - 1)
    def _():
        wait_rows(out_hbm, orow, sems.at[1])
        g_t = gate_ref[0].T
        for j in range(tm // LANES):
            rs = slice(j * LANES, (j + 1) * LANES)
            orow[rs, :] = orow[rs, :] + acc[rs, :] * g_t[:, j:j + 1]

        def body(s, _):
            tok = idx_cur[0, 0, s]
            pltpu.make_async_copy(orow.at[pl.ds(s, 1)], out_hbm.at[pl.ds(tok, 1)], sems.at[2]).start()
            return 0
        lax.fori_loop(0, tm, body, 0)

        @pl.when(n == n_tiles - 1)
        def _():
            pltpu.make_async_copy(orow, out_hbm.at[pl.ds(0, tm)], sems.at[2]).wait()


def _moe(idx, gate, h2, x1, wg, wu, wd, tm, tf):
    n_exp, slots = idx.shape
    ntok, d = x1.shape
    ff = wg.shape[2]
    nt = slots // tm
    n_f = ff // tf
    n_tiles = n_exp * nt
    idx3 = idx.reshape(n_tiles, 1, tm)
    gate3 = gate.reshape(n_exp, slots // LANES, LANES)
    smem = pltpu.MemorySpace.SMEM
    in_specs = [
        pl.BlockSpec((1, 1, tm), lambda e, t, f: (e * nt + t, 0, 0), memory_space=smem),
        pl.BlockSpec((1, 1, tm), lambda e, t, f: (jnp.minimum(e * nt + t + 1, n_tiles - 1), 0, 0),
                     memory_space=smem),
        pl.BlockSpec((1, tm // LANES, LANES), lambda e, t, f: (e, t, 0)),
        pl.BlockSpec((1, d, tf), lambda e, t, f: (e, 0, f)),
        pl.BlockSpec((1, d, tf), lambda e, t, f: (e, 0, f)),
        pl.BlockSpec((1, tf, d), lambda e, t, f: (e, f, 0)),
        pl.BlockSpec(memory_space=pl.ANY),
        pl.BlockSpec(memory_space=pl.ANY),
    ]
    return pl.pallas_call(
        functools.partial(_moe_kernel, tm=tm, n_tiles=n_tiles, n_f=n_f),
        grid=(n_exp, nt, n_f), in_specs=in_specs,
        out_specs=pl.BlockSpec(memory_space=pl.ANY),
        out_shape=jax.ShapeDtypeStruct((ntok, d), jnp.float32),
        scratch_shapes=[
            pltpu.VMEM((tm, d), jnp.float32),
            pltpu.VMEM((tm, d), jnp.bfloat16),
            pltpu.VMEM((tm, d), jnp.float32),
            pltpu.VMEM((tm, d), jnp.float32),
            pltpu.SemaphoreType.DMA((3,)),
        ],
        input_output_aliases={7: 0},
        compiler_params=_cparams(("arbitrary", "arbitrary", "arbitrary")),
        name="expert_ffn",
    )(idx3, idx3, gate3, wg, wu, wd, h2, x1)


def _pick(n, candidates):
    for c in candidates:
        if n % c == 0:
            return c
    raise ValueError(f"no tile among {candidates} divides {n}")


def _layer(x, np_tok, seq_p, seq_s, rel_table, norm_mix_g, w_in, q_norm_a, k_norm_a, q_norm_b,
           k_norm_b, sink_b, w_proj_a, w_proj_b, w_out, norm_ffn_g, w_router, w_gate_e, w_up_e,
           w_down_e):
    ntok, d = x.shape
    ns_tok = ntok - np_tok
    bf = jnp.bfloat16
    z = _inproj(x, norm_mix_g, w_in.astype(bf), _pick(ntok, (1024, 512, 256)),
                _pick(w_in.shape[1], (1024, 512)))

    oas, lses = [], []
    for g in range(N_DIL_GROUPS):
        dil = DIL_RATES[g]
        radius = (DIL_WINDOWS[g] // 2) // dil
        tq = min(128, seq_p // dil, seq_s // dil)
        bias = _bias_tile(rel_table[:, g * A_HEADS:(g + 1) * A_HEADS], dil, radius, tq)
        o, lse = _banded_attention(
            z, dil=dil, radius=radius, tq=tq, q_col=g * GROUP_COLS, k_col=A_QKV + g * GROUP_COLS,
            v_col=2 * A_QKV + g * GROUP_COLS, kv_width=GROUP_COLS, n_par=dil, bias=bias,
            q_w=q_norm_a, k_w=k_norm_a, sink=None, with_lse=True, np_tok=np_tok, seq_p=seq_p,
            seq_s=seq_s, out_cols=A_OUT)
        oas.append(o)
        lses.append(lse)
    tq_b = 256
    bias_b = _bias_tile(rel_table[:, N_DIL_GROUPS * A_HEADS:], 1, B_RADIUS, tq_b)
    ob = _banded_attention(
        z, dil=1, radius=B_RADIUS, tq=tq_b, q_col=3 * A_QKV, k_col=3 * A_QKV + B_Q,
        v_col=3 * A_QKV + B_Q + B_KV, kv_width=HEAD_DIM, n_par=B_KV_HEADS, bias=bias_b,
        q_w=q_norm_b, k_w=k_norm_b, sink=sink_b, with_lse=False, np_tok=np_tok, seq_p=seq_p,
        seq_s=seq_s, out_cols=B_Q)

    x1, h2, aff = _merge(x, oas, lses, ob, z, w_proj_a.astype(bf), w_proj_b.astype(bf),
                         w_out.astype(bf), norm_ffn_g, w_router.T.astype(bf),
                         _pick(ntok, (256,)))

    n_exp = w_router.shape[1]
    cap_p = max(1, EC_CAPACITY * np_tok // n_exp)
    cap_s = max(1, EC_CAPACITY * ns_tok // n_exp)
    idx_p, gate_p = _route_set(aff[:np_tok // LANES], cap_p, 0)
    idx_s, gate_s = _route_set(aff[np_tok // LANES:], cap_s, np_tok)
    idx = jnp.concatenate([idx_p, idx_s], axis=1)
    gate = jnp.concatenate([gate_p, gate_s], axis=1)
    tm = _pick(math.gcd(cap_p, cap_s), (1024,))
    return _moe(idx, gate, h2, x1, w_gate_e, w_up_e, w_down_e, tm, _pick(w_gate_e.shape[2], (256, 128)))


def kernel(x_prompt, x_sample, rel_table, norm_mix_g, w_in, q_norm_a, k_norm_a, q_norm_b, k_norm_b,
           sink_b, w_proj_a, w_proj_b, w_out, norm_ffn_g, w_router, w_gate_e, w_up_e, w_down_e):
    bp, sp, d = x_prompt.shape
    bs, ss, _ = x_sample.shape
    np_tok = bp * sp
    x = jnp.concatenate([x_prompt.reshape(np_tok, d), x_sample.reshape(bs * ss, d)], axis=0)
    for l in range(norm_mix_g.shape[0]):
        x = _layer(x, np_tok, sp, ss, rel_table, norm_mix_g[l], w_in[l], q_norm_a[l], k_norm_a[l],
                   q_norm_b[l], k_norm_b[l], sink_b[l], w_proj_a[l], w_proj_b[l], w_out[l],
                   norm_ffn_g[l], w_router[l], w_gate_e[l], w_up_e[l], w_down_e[l])
    return x[:np_tok].reshape(bp, sp, d), x[np_tok:].reshape(bs, ss, d)
```

```python
import functools
import math

import jax
import jax.numpy as jnp
from jax import lax
from jax.experimental import pallas as pl
from jax.experimental.pallas import tpu as pltpu

HEAD_DIM = 128
DIL_WINDOWS = (128, 512, 2048)
DIL_RATES = (1, 4, 16)
N_DIL_GROUPS = 3
A_HEADS = 4
B_Q_HEADS = 8
B_KV_HEADS = 2
B_GROUP = B_Q_HEADS // B_KV_HEADS
B_RADIUS = 128
REL_BUCKETS = 32
REL_MAX_DIST = 1024
EC_CAPACITY = 2
NORM_EPS = 1e-6
NEG_INF = -1e30

A_QKV = N_DIL_GROUPS * A_HEADS * HEAD_DIM
A_OUT = A_HEADS * HEAD_DIM
B_Q = B_Q_HEADS * HEAD_DIM
B_KV = B_KV_HEADS * HEAD_DIM

LANES = 128
GROUP_COLS = A_HEADS * HEAD_DIM
VMEM_LIMIT = 60 * 1024 * 1024
COMPACT_CHUNKS = 8
ROW_DMA_UNROLL = 8


def _cparams(sem, vmem=VMEM_LIMIT):
    return pltpu.CompilerParams(dimension_semantics=sem, vmem_limit_bytes=vmem)


def _inproj_kernel(x_ref, g_ref, w_ref, z_ref, h_scr, *zs_scr, dil, tm):
    @pl.when(pl.program_id(1) == 0)
    def _():
        x = x_ref[...]
        ms = jnp.mean(x * x, axis=-1, keepdims=True)
        h_scr[...] = (x * lax.rsqrt(ms + NORM_EPS) * g_ref[...]).astype(jnp.bfloat16)

    z = jnp.dot(h_scr[...], w_ref[...], preferred_element_type=jnp.float32)
    if dil == 1:
        z_ref[...] = z.astype(jnp.bfloat16)
    else:
        zs = zs_scr[0]
        tn = z.shape[1]
        rows = tm // dil
        for c in range(tn // LANES):
            zs[c] = z[:, c * LANES:(c + 1) * LANES]
        for r in range(dil):
            for c in range(tn // LANES):
                z_ref[:, r * tn + c * LANES:r * tn + (c + 1) * LANES] = (
                    zs[c, pl.ds(r, rows, stride=dil), :].astype(jnp.bfloat16))


def _inproj(x, g, w_bf16, tm, tn, dil=1):
    n, d = x.shape
    cols = w_bf16.shape[1]
    assert dil == 1 or tn == cols
    return pl.pallas_call(
        functools.partial(_inproj_kernel, dil=dil, tm=tm),
        grid=(n // tm, cols // tn),
        in_specs=[
            pl.BlockSpec((tm, d), lambda i, j: (i, 0)),
            pl.BlockSpec((1, d), lambda i, j: (0, 0)),
            pl.BlockSpec((d, tn), lambda i, j: (0, j)),
        ],
        out_specs=pl.BlockSpec((tm // dil, dil * tn), lambda i, j: (i, j)),
        out_shape=jax.ShapeDtypeStruct((n // dil, dil * cols), jnp.bfloat16),
        scratch_shapes=[pltpu.VMEM((tm, d), jnp.bfloat16)]
        + ([pltpu.VMEM((tn // LANES, tm, LANES), jnp.float32)] if dil > 1 else []),
        compiler_params=_cparams(("parallel", "arbitrary")),
        name=f"inproj_d{dil}",
    )(x, g.reshape(1, d), w_bf16)


def _t5_bucket(rel):
    half = REL_BUCKETS // 2
    max_exact = half // 2
    n = jnp.abs(rel)
    base = jnp.where(rel > 0, half, 0)
    nf = jnp.maximum(n, 1).astype(jnp.float32)
    large = max_exact + (jnp.log(nf / max_exact) / math.log(REL_MAX_DIST / max_exact)
                         * (half - max_exact)).astype(jnp.int32)
    large = jnp.minimum(large, half - 1)
    return base + jnp.where(n < max_exact, n, large)


def _bias_tile(table_cols, dil, radius, tq):
    tk = tq + 2 * radius
    n_heads = table_cols.shape[1]
    rel = jnp.arange(-radius, radius + 1)
    vals = table_cols[_t5_bucket(rel * dil)].astype(jnp.float32).T
    period = tq + tk
    w = jnp.full((n_heads, period), NEG_INF, jnp.float32).at[:, :2 * radius + 1].set(vals)
    flat = jnp.tile(w, (1, tq))[:, :tq * (period - 1)]
    return flat.reshape(n_heads, tq, period - 1)[:, :, :tk]


def _head_norm(x, w):
    xf = x.astype(jnp.float32)
    ms = jnp.mean(xf * xf, axis=-1, keepdims=True)
    return xf * lax.rsqrt(ms + NORM_EPS) * w


def _attn_kernel(*refs, tq, radius, shared_kv, with_sink, with_lse, np_rows, len_p, len_s):
    (q_ref, kp_ref, kc_ref, kn_ref, vp_ref, vc_ref, vn_ref, bias_ref, qw_ref, kw_ref) = refs[:10]
    pos = 10
    sink_ref = None
    if with_sink:
        sink_ref = refs[pos]
        pos += 1
    o_ref = refs[pos]
    lse_ref = refs[pos + 1] if with_lse else None

    tk = tq + 2 * radius
    q0 = pl.program_id(1) * tq
    lo_p = (q0 // len_p) * len_p
    lo_s = np_rows + ((q0 - np_rows) // len_s) * len_s
    in_p = q0 < np_rows
    lo = jnp.where(in_p, lo_p, lo_s)
    hi = lo + jnp.where(in_p, len_p, len_s)
    kpos = q0 - radius + lax.broadcasted_iota(jnp.int32, (1, tk), 1)
    valid = (kpos >= lo) & (kpos < hi)

    scale = HEAD_DIM ** -0.5
    k_all = jnp.concatenate([kp_ref[...], kc_ref[...], kn_ref[...]], axis=0)
    v_all = jnp.concatenate([vp_ref[...], vc_ref[...], vn_ref[...]], axis=0)
    qw = qw_ref[...]
    kw = kw_ref[...]
    if shared_kv:
        k_sh = _head_norm(k_all, kw).astype(jnp.bfloat16)

    lane = lax.broadcasted_iota(jnp.int32, (tq, LANES), 1)
    lse_tile = jnp.zeros((tq, LANES), jnp.float32)
    for h in range(A_HEADS):
        cs = slice(h * HEAD_DIM, (h + 1) * HEAD_DIM)
        qh = (_head_norm(q_ref[:, cs], qw) * scale).astype(jnp.bfloat16)
        if shared_kv:
            kh, vh = k_sh, v_all
        else:
            kh = _head_norm(k_all[:, cs], kw).astype(jnp.bfloat16)
            vh = v_all[:, cs]
        s = lax.dot_general(qh, kh, (((1,), (1,)), ((), ())),
                            preferred_element_type=jnp.float32)
        s = jnp.where(valid, s + bias_ref[h], NEG_INF)
        m = jnp.max(s, axis=-1, keepdims=True)
        if with_sink:
            m = jnp.maximum(m, sink_ref[h])
        p = jnp.exp(s - m)
        l = jnp.sum(p, axis=-1, keepdims=True)
        if with_sink:
            l = l + jnp.exp(sink_ref[h] - m)
        o = jnp.dot(p.astype(jnp.bfloat16), vh, preferred_element_type=jnp.float32) / l
        o_ref[:, cs] = o.astype(o_ref.dtype)
        if with_lse:
            lse_tile = jnp.where(lane == h, m + jnp.log(l), lse_tile)
    if with_lse:
        lse_ref[...] = lse_tile


def _banded_attention(zv, *, dil, radius, tq, q_col, k_col, v_col, kv_width, n_par, bias,
                      q_w, k_w, sink, with_lse, np_tok, seq_p, seq_s, out_cols):
    rows = zv.shape[0]
    in_cols = zv.shape[1] // dil
    ntok = rows * dil
    nq = rows // tq
    hb = tq // radius
    n_halo = rows // radius
    shared_kv = kv_width == HEAD_DIM
    q_blk = in_cols // GROUP_COLS
    kv_blk = in_cols // kv_width
    if dil > 1:
        qmap = lambda r, i: (i, r * q_blk + q_col // GROUP_COLS)
        kc = lambda c: (lambda r, i: (i, r * kv_blk + c // kv_width))
        kp = lambda c: (lambda r, i: (jnp.maximum(i * hb - 1, 0), r * kv_blk + c // kv_width))
        kn = lambda c: (lambda r, i: (jnp.minimum((i + 1) * hb, n_halo - 1), r * kv_blk + c // kv_width))
        omap = lambda r, i: (i, r)
    else:
        qmap = lambda r, i: (i, q_col // GROUP_COLS + r)
        kc = lambda c: (lambda r, i: (i, c // kv_width + r))
        kp = lambda c: (lambda r, i: (jnp.maximum(i * hb - 1, 0), c // kv_width + r))
        kn = lambda c: (lambda r, i: (jnp.minimum((i + 1) * hb, n_halo - 1), c // kv_width + r))
        omap = lambda r, i: (i, r)

    tk = tq + 2 * radius
    in_specs = [
        pl.BlockSpec((tq, GROUP_COLS), qmap),
        pl.BlockSpec((radius, kv_width), kp(k_col)),
        pl.BlockSpec((tq, kv_width), kc(k_col)),
        pl.BlockSpec((radius, kv_width), kn(k_col)),
        pl.BlockSpec((radius, kv_width), kp(v_col)),
        pl.BlockSpec((tq, kv_width), kc(v_col)),
        pl.BlockSpec((radius, kv_width), kn(v_col)),
        pl.BlockSpec((A_HEADS, tq, tk), (lambda r, i: (r, 0, 0)) if (dil == 1 and n_par > 1)
                     else (lambda r, i: (0, 0, 0))),
        pl.BlockSpec((1, HEAD_DIM), lambda r, i: (0, 0)),
        pl.BlockSpec((1, HEAD_DIM), lambda r, i: (0, 0)),
    ]
    args = [zv] * 7 + [bias, q_w.reshape(1, HEAD_DIM), k_w.reshape(1, HEAD_DIM)]
    if sink is not None:
        in_specs.append(pl.BlockSpec((A_HEADS, 1, 1), lambda r, i: (r, 0, 0)))
        args.append(sink.reshape(-1, 1, 1).astype(jnp.float32))
    out_specs = [pl.BlockSpec((tq, GROUP_COLS), omap)]
    out_shape = [jax.ShapeDtypeStruct((rows, n_par * GROUP_COLS), jnp.bfloat16)]
    if with_lse:
        out_specs.append(pl.BlockSpec((tq, LANES), omap))
        out_shape.append(jax.ShapeDtypeStruct((rows, n_par * LANES), jnp.float32))
    kern = functools.partial(
        _attn_kernel, tq=tq, radius=radius, shared_kv=shared_kv, with_sink=sink is not None,
        with_lse=with_lse, np_rows=np_tok // dil, len_p=seq_p // dil, len_s=seq_s // dil)
    outs = pl.pallas_call(
        kern, grid=(n_par, nq), in_specs=in_specs, out_specs=out_specs, out_shape=out_shape,
        compiler_params=_cparams(("parallel", "arbitrary")),
        name=f"band_attn_d{dil}_r{radius}",
    )(*args)
    o = outs[0].reshape(ntok, out_cols)
    if with_lse:
        return o, outs[1].reshape(ntok, LANES)
    return o


def _merge_kernel(x_ref, oa0_ref, oa1_ref, oa2_ref, l0_ref, l1_ref, l2_ref, ob_ref, ga_ref, gb_ref,
                  wpa_ref, wpb_ref, wo_ref, g2_ref, wr_ref, x1_ref, h2_ref, aff_ref, *, tm):
    l0, l1, l2 = l0_ref[...], l1_ref[...], l2_ref[...]
    mx = jnp.maximum(jnp.maximum(l0, l1), l2)
    e0, e1, e2 = jnp.exp(l0 - mx), jnp.exp(l1 - mx), jnp.exp(l2 - mx)
    den = e0 + e1 + e2
    w0, w1, w2 = e0 / den, e1 / den, e2 / den
    parts = []
    for h in range(A_HEADS):
        cs = slice(h * HEAD_DIM, (h + 1) * HEAD_DIM)
        parts.append(w0[:, h:h + 1] * oa0_ref[:, cs].astype(jnp.float32)
                     + w1[:, h:h + 1] * oa1_ref[:, cs].astype(jnp.float32)
                     + w2[:, h:h + 1] * oa2_ref[:, cs].astype(jnp.float32))
    o_a = jnp.concatenate(parts, axis=1).astype(jnp.bfloat16)
    pa = jnp.dot(o_a, wpa_ref[...], preferred_element_type=jnp.float32)
    pb = jnp.dot(ob_ref[...], wpb_ref[...], preferred_element_type=jnp.float32)
    merged = (jax.nn.sigmoid(ga_ref[...].astype(jnp.float32)) * pa
              + jax.nn.sigmoid(gb_ref[...].astype(jnp.float32)) * pb)
    x1 = x_ref[...] + jnp.dot(merged.astype(jnp.bfloat16), wo_ref[...],
                              preferred_element_type=jnp.float32)
    x1_ref[...] = x1
    ms = jnp.mean(x1 * x1, axis=-1, keepdims=True)
    h2 = x1 * lax.rsqrt(ms + NORM_EPS) * g2_ref[...]
    h2_ref[...] = h2
    logits = lax.dot_general(wr_ref[...], h2.astype(jnp.bfloat16), (((1,), (1,)), ((), ())),
                             preferred_element_type=jnp.float32)
    mx2 = jnp.max(logits, axis=0, keepdims=True)
    ex = jnp.exp(logits - mx2)
    aff = ex / jnp.sum(ex, axis=0, keepdims=True)
    for j in range(tm // LANES):
        aff_ref[j] = aff[:, j * LANES:(j + 1) * LANES]


def _merge(x, oas, lses, ob, z, wpa, wpb, wo, g2, wr_t, tm):
    n, d = x.shape
    n_exp = wr_t.shape[0]
    ga_blk = 0
    row = lambda i: (i, 0)
    const = lambda i: (0, 0)
    in_specs = [
        pl.BlockSpec((tm, d), row),
        pl.BlockSpec((tm, A_OUT), row), pl.BlockSpec((tm, A_OUT), row), pl.BlockSpec((tm, A_OUT), row),
        pl.BlockSpec((tm, LANES), row), pl.BlockSpec((tm, LANES), row), pl.BlockSpec((tm, LANES), row),
        pl.BlockSpec((tm, B_Q), row),
        pl.BlockSpec((tm, d), lambda i: (i, ga_blk)),
        pl.BlockSpec((tm, d), lambda i: (i, ga_blk + 1)),
        pl.BlockSpec((A_OUT, d), const),
        pl.BlockSpec((B_Q, d), const),
        pl.BlockSpec((d, d), const),
        pl.BlockSpec((1, d), const),
        pl.BlockSpec((n_exp, d), const),
    ]
    out_specs = [
        pl.BlockSpec((tm, d), row),
        pl.BlockSpec((tm, d), row),
        pl.BlockSpec((tm // LANES, n_exp, LANES), lambda i: (i, 0, 0)),
    ]
    out_shape = [
        jax.ShapeDtypeStruct((n, d), jnp.float32),
        jax.ShapeDtypeStruct((n, d), jnp.float32),
        jax.ShapeDtypeStruct((n // LANES, n_exp, LANES), jnp.float32),
    ]
    return pl.pallas_call(
        functools.partial(_merge_kernel, tm=tm),
        grid=(n // tm,), in_specs=in_specs, out_specs=out_specs, out_shape=out_shape,
        compiler_params=_cparams(("parallel",)),
        name="merge_proj_router",
    )(x, oas[0], oas[1], oas[2], lses[0], lses[1], lses[2], ob, z, z, wpa, wpb, wo,
      g2.reshape(1, d), wr_t)


def _prefix_counts(flag_f32, tri, tot_scr, off_scr, nc):
    n_exp = flag_f32.shape[1]
    incl = jnp.dot(flag_f32.astype(jnp.bfloat16).reshape(nc * n_exp, LANES), tri,
                   preferred_element_type=jnp.float32).reshape(nc, n_exp, LANES)
    tot_scr[...] = jnp.broadcast_to(incl[:, :, LANES - 1:LANES], (nc, n_exp, LANES))

    def body(c, run):
        off_scr[c] = run
        return run + tot_scr[c]

    lax.fori_loop(0, nc, body, jnp.zeros((n_exp, LANES), jnp.float32))
    return off_scr[...] + incl - flag_f32


def _route_kernel(aff_ref, pos_ref, off_ref, tot_scr, off_scr, *, cap, nc):
    n_exp = aff_ref.shape[1]
    capf = jnp.float32(cap)

    def count(mask):
        c = jnp.sum(jnp.where(mask, 1.0, 0.0), axis=0)
        return jnp.sum(c, axis=-1, keepdims=True)

    def bit_body(k, t):
        cand = t | jnp.left_shift(jnp.int32(1), 30 - k)
        bits = pltpu.bitcast(aff_ref[...], jnp.int32)
        return jnp.where(count(bits >= cand[None]) >= capf, cand, t)

    t = lax.fori_loop(0, 31, bit_body, jnp.zeros((n_exp, 1), jnp.int32))
    bits = pltpu.bitcast(aff_ref[...], jnp.int32)
    gt = bits > t[None]
    eq = bits == t[None]
    need = capf - count(gt)
    rows = lax.broadcasted_iota(jnp.int32, (LANES, LANES), 0)
    cols = lax.broadcasted_iota(jnp.int32, (LANES, LANES), 1)
    tri = jnp.where(rows <= cols, 1.0, 0.0).astype(jnp.bfloat16)
    eq_f = jnp.where(eq, 1.0, 0.0)
    tie_rank = _prefix_counts(eq_f, tri, tot_scr, off_scr, nc)
    sel = gt | (eq & (tie_rank < need[None]))
    sel_f = jnp.where(sel, 1.0, 0.0)
    slot = _prefix_counts(sel_f, tri, tot_scr, off_scr, nc)
    pos_ref[...] = jnp.where(sel, slot, -1.0).astype(jnp.int32)
    off_ref[...] = off_scr[...].astype(jnp.int32)


def _route(aff, cap):
    nc, n_exp, _ = aff.shape
    full = pl.BlockSpec((nc, n_exp, LANES), lambda: (0, 0, 0))
    return pl.pallas_call(
        functools.partial(_route_kernel, cap=cap, nc=nc),
        in_specs=[full], out_specs=[full, full],
        out_shape=[jax.ShapeDtypeStruct((nc, n_exp, LANES), jnp.int32)] * 2,
        scratch_shapes=[pltpu.VMEM((nc, n_exp, LANES), jnp.float32)] * 2,
        compiler_params=pltpu.CompilerParams(vmem_limit_bytes=VMEM_LIMIT),
        name="route_select",
    )(aff)


def _split3(x):
    a = x.astype(jnp.bfloat16).astype(jnp.float32)
    r = x - a
    b = r.astype(jnp.bfloat16).astype(jnp.float32)
    c = r - b
    return a, b, c


def _compact_kernel(off_smem, pos_ref, aff_ref, idx_ref, gate_ref, acc_scr, *, nc, n_blk, tok_base):
    e = pl.program_id(0)
    stride = nc + 1
    sub = lax.broadcasted_iota(jnp.int32, (LANES, LANES), 0)
    sub16 = lax.broadcasted_iota(jnp.int32, (16, LANES), 0)
    lane = lax.broadcasted_iota(jnp.int32, (1, LANES), 1)

    n_grp = nc // COMPACT_CHUNKS

    def chunk_off(c):
        return off_smem[e * stride + jnp.minimum(c, nc)]

    def block_body(sb, g_first):
        lo_slot = sb * LANES

        def skip_cond(g):
            return jnp.logical_and(g < n_grp, chunk_off((g + 1) * COMPACT_CHUNKS) <= lo_slot)

        g_first = lax.while_loop(skip_cond, lambda g: g + 1, g_first)
        acc_scr[...] = jnp.zeros_like(acc_scr)

        def take_cond(g):
            return jnp.logical_and(g < n_grp, chunk_off(g * COMPACT_CHUNKS) < lo_slot + LANES)

        def take(g):
            total = jnp.zeros((16, LANES), jnp.float32)
            for k in range(COMPACT_CHUNKS):
                c = g * COMPACT_CHUNKS + k
                rel = pos_ref[c, pl.ds(e, 1), :] - lo_slot
                onehot = jnp.where(sub == rel, 1.0, 0.0).astype(jnp.bfloat16)
                tok = tok_base + c * LANES + lane
                g1, g2, g3 = _split3(aff_ref[c, pl.ds(e, 1), :])
                lhs = jnp.where(sub16 == 0, (tok >> 8).astype(jnp.float32),
                      jnp.where(sub16 == 1, (tok & 255).astype(jnp.float32),
                      jnp.where(sub16 == 2, g1,
                      jnp.where(sub16 == 3, g2,
                      jnp.where(sub16 == 4, g3, 0.0))))).astype(jnp.bfloat16)
                total = total + lax.dot_general(lhs, onehot, (((1,), (1,)), ((), ())),
                                                preferred_element_type=jnp.float32)
            acc_scr[...] += total
            return g + 1

        lax.while_loop(take_cond, take, g_first)
        acc = acc_scr[...]
        idx_ref[0, pl.ds(sb, 1), :] = (acc[0:1] * 256.0 + acc[1:2]).astype(jnp.int32)
        gate_ref[0, pl.ds(sb, 1), :] = (acc[2:3] + acc[3:4]) + acc[4:5]
        return g_first

    lax.fori_loop(0, n_blk, block_body, jnp.int32(0))


def _compact(offs_flat, pos, aff, cap, tok_base):
    nc, n_exp, _ = pos.shape
    n_blk = cap // LANES
    full = pl.BlockSpec((nc, n_exp, LANES), lambda e, off: (0, 0, 0))
    out = pl.BlockSpec((1, n_blk, LANES), lambda e, off: (e, 0, 0))
    return pl.pallas_call(
        functools.partial(_compact_kernel, nc=nc, n_blk=n_blk, tok_base=tok_base),
        grid_spec=pltpu.PrefetchScalarGridSpec(
            num_scalar_prefetch=1, grid=(n_exp,), in_specs=[full, full], out_specs=[out, out],
            scratch_shapes=[pltpu.VMEM((16, LANES), jnp.float32)]),
        out_shape=[jax.ShapeDtypeStruct((n_exp, n_blk, LANES), jnp.int32),
                   jax.ShapeDtypeStruct((n_exp, n_blk, LANES), jnp.float32)],
        compiler_params=_cparams(("arbitrary",)),
        name="route_compact",
    )(offs_flat, pos, aff)


def _route_set(aff, cap, tok_base):
    nc, n_exp, _ = aff.shape
    pos, off = _route(aff, cap)
    offs = jnp.concatenate([off[:, :, 0].T, jnp.full((n_exp, 1), cap, jnp.int32)], axis=1)
    idx, gate = _compact(offs.reshape(-1), pos, aff, cap, tok_base)
    return idx.reshape(n_exp, cap), gate.reshape(n_exp, cap)


def _moe_kernel(idx_cur, idx_nxt, gate_ref, wg_ref, wu_ref, wd_ref, h2_hbm, x1_hbm, out_hbm,
                xe32, xe16, acc, orow, sems, *, tm, n_tiles, n_f):
    del x1_hbm
    e, t, f = pl.program_id(0), pl.program_id(1), pl.program_id(2)
    n = e * pl.num_programs(1) + t

    def gather_rows(src_hbm, idx_smem, dst, sem):
        def body(s, _):
            tok = idx_smem[0, 0, s]
            pltpu.make_async_copy(src_hbm.at[pl.ds(tok, 1)], dst.at[pl.ds(s, 1)], sem).start()
            return 0
        lax.fori_loop(0, tm, body, 0, unroll=ROW_DMA_UNROLL)

    def wait_rows(src_hbm, dst, sem):
        pltpu.make_async_copy(src_hbm.at[pl.ds(0, tm)], dst, sem).wait()

    @pl.when(f == 0)
    def _():
        @pl.when(n == 0)
        def _():
            gather_rows(h2_hbm, idx_cur, xe32, sems.at[0])

        wait_rows(h2_hbm, xe32, sems.at[0])
        xe16[...] = xe32[...].astype(jnp.bfloat16)

        @pl.when(n + 1 < n_tiles)
        def _():
            gather_rows(h2_hbm, idx_nxt, xe32, sems.at[0])

        @pl.when(n > 0)
        def _():
            pltpu.make_async_copy(orow, out_hbm.at[pl.ds(0, tm)], sems.at[2]).wait()

        gather_rows(out_hbm, idx_cur, orow, sems.at[1])
        acc[...] = jnp.zeros_like(acc)

    x = xe16[...]
    gp = jnp.dot(x, wg_ref[0].astype(jnp.bfloat16), preferred_element_type=jnp.float32)
    up = jnp.dot(x, wu_ref[0].astype(jnp.bfloat16), preferred_element_type=jnp.float32)
    hid = (gp * jax.nn.sigmoid(gp) * up).astype(jnp.bfloat16)
    part = jnp.dot(hid, wd_ref[0].astype(jnp.bfloat16), preferred_element_type=jnp.float32)

    acc[...] += part

    @pl.when(f == n_f - 1)
    def _():
        wait_rows(out_hbm, orow, sems.at[1])
        g_t = gate_ref[0].T
        for j in range(tm // LANES):
            rs = slice(j * LANES, (j + 1) * LANES)
            orow[rs, :] = orow[rs, :] + acc[rs, :] * g_t[:, j:j + 1]

        def body(s, _):
            tok = idx_cur[0, 0, s]
            pltpu.make_async_copy(orow.at[pl.ds(s, 1)], out_hbm.at[pl.ds(tok, 1)], sems.at[2]).start()
            return 0
        lax.fori_loop(0, tm, body, 0, unroll=ROW_DMA_UNROLL)

        @pl.when(n == n_tiles - 1)
        def _():
            pltpu.make_async_copy(orow, out_hbm.at[pl.ds(0, tm)], sems.at[2]).wait()


def _moe(idx, gate, h2, x1, wg, wu, wd, tm, tf):
    n_exp, slots = idx.shape
    ntok, d = x1.shape
    ff = wg.shape[2]
    nt = slots // tm
    n_f = ff // tf
    n_tiles = n_exp * nt
    idx3 = idx.reshape(n_tiles, 1, tm)
    gate3 = gate.reshape(n_exp, slots // LANES, LANES)
    smem = pltpu.MemorySpace.SMEM
    in_specs = [
        pl.BlockSpec((1, 1, tm), lambda e, t, f: (e * nt + t, 0, 0), memory_space=smem),
        pl.BlockSpec((1, 1, tm), lambda e, t, f: (jnp.minimum(e * nt + t + 1, n_tiles - 1), 0, 0),
                     memory_space=smem),
        pl.BlockSpec((1, tm // LANES, LANES), lambda e, t, f: (e, t, 0)),
        pl.BlockSpec((1, d, tf), lambda e, t, f: (e, 0, f)),
        pl.BlockSpec((1, d, tf), lambda e, t, f: (e, 0, f)),
        pl.BlockSpec((1, tf, d), lambda e, t, f: (e, f, 0)),
        pl.BlockSpec(memory_space=pl.ANY),
        pl.BlockSpec(memory_space=pl.ANY),
    ]
    return pl.pallas_call(
        functools.partial(_moe_kernel, tm=tm, n_tiles=n_tiles, n_f=n_f),
        grid=(n_exp, nt, n_f), in_specs=in_specs,
        out_specs=pl.BlockSpec(memory_space=pl.ANY),
        out_shape=jax.ShapeDtypeStruct((ntok, d), jnp.float32),
        scratch_shapes=[
            pltpu.VMEM((tm, d), jnp.float32),
            pltpu.VMEM((tm, d), jnp.bfloat16),
            pltpu.VMEM((tm, d), jnp.float32),
            pltpu.VMEM((tm, d), jnp.float32),
            pltpu.SemaphoreType.DMA((3,)),
        ],
        input_output_aliases={7: 0},
        compiler_params=_cparams(("arbitrary", "arbitrary", "arbitrary")),
        name="expert_ffn",
    )(idx3, idx3, gate3, wg, wu, wd, h2, x1)


def _pick(n, candidates):
    for c in candidates:
        if n % c == 0:
            return c
    raise ValueError(f"no tile among {candidates} divides {n}")


def _layer(x, np_tok, seq_p, seq_s, rel_table, norm_mix_g, w_in, q_norm_a, k_norm_a, q_norm_b,
           k_norm_b, sink_b, w_proj_a, w_proj_b, w_out, norm_ffn_g, w_router, w_gate_e, w_up_e,
           w_down_e):
    ntok, d = x.shape
    ns_tok = ntok - np_tok
    bf = jnp.bfloat16
    tm_in = _pick(ntok, (1024, 512, 256))

    def group_cols(g):
        return [w_in[:, s * A_QKV + g * GROUP_COLS: s * A_QKV + (g + 1) * GROUP_COLS] for s in range(3)]

    w_nat = jnp.concatenate([w_in[:, 3 * A_QKV + B_Q + 2 * B_KV:]] + group_cols(0)
                            + [w_in[:, 3 * A_QKV:3 * A_QKV + B_Q + 2 * B_KV]], axis=1).astype(bf)
    z = _inproj(x, norm_mix_g, w_nat, tm_in, _pick(w_nat.shape[1], (1024, 512)))
    a0 = 2 * d
    b0 = a0 + 3 * GROUP_COLS

    oas, lses = [], []
    for g in range(N_DIL_GROUPS):
        dil = DIL_RATES[g]
        radius = (DIL_WINDOWS[g] // 2) // dil
        tq = min(128, seq_p // dil, seq_s // dil)
        bias = _bias_tile(rel_table[:, g * A_HEADS:(g + 1) * A_HEADS], dil, radius, tq)
        if dil == 1:
            zv, c0 = z, a0
        else:
            w_g = jnp.concatenate(group_cols(g), axis=1).astype(bf)
            zv, c0 = _inproj(x, norm_mix_g, w_g, tm_in, w_g.shape[1], dil=dil), 0
        o, lse = _banded_attention(
            zv, dil=dil, radius=radius, tq=tq, q_col=c0, k_col=c0 + GROUP_COLS,
            v_col=c0 + 2 * GROUP_COLS, kv_width=GROUP_COLS, n_par=dil, bias=bias,
            q_w=q_norm_a, k_w=k_norm_a, sink=None, with_lse=True, np_tok=np_tok, seq_p=seq_p,
            seq_s=seq_s, out_cols=A_OUT)
        oas.append(o)
        lses.append(lse)
    tq_b = 256
    bias_b = _bias_tile(rel_table[:, N_DIL_GROUPS * A_HEADS:], 1, B_RADIUS, tq_b)
    ob = _banded_attention(
        z, dil=1, radius=B_RADIUS, tq=tq_b, q_col=b0, k_col=b0 + B_Q,
        v_col=b0 + B_Q + B_KV, kv_width=HEAD_DIM, n_par=B_KV_HEADS, bias=bias_b,
        q_w=q_norm_b, k_w=k_norm_b, sink=sink_b, with_lse=False, np_tok=np_tok, seq_p=seq_p,
        seq_s=seq_s, out_cols=B_Q)

    x1, h2, aff = _merge(x, oas, lses, ob, z, w_proj_a.astype(bf), w_proj_b.astype(bf),
                         w_out.astype(bf), norm_ffn_g, w_router.T.astype(bf),
                         _pick(ntok, (256,)))

    n_exp = w_router.shape[1]
    cap_p = max(1, EC_CAPACITY * np_tok // n_exp)
    cap_s = max(1, EC_CAPACITY * ns_tok // n_exp)
    idx_p, gate_p = _route_set(aff[:np_tok // LANES], cap_p, 0)
    idx_s, gate_s = _route_set(aff[np_tok // LANES:], cap_s, np_tok)
    idx = jnp.concatenate([idx_p, idx_s], axis=1)
    gate = jnp.concatenate([gate_p, gate_s], axis=1)
    tm = _pick(math.gcd(cap_p, cap_s), (1024,))
    return _moe(idx, gate, h2, x1, w_gate_e, w_up_e, w_down_e, tm, _pick(w_gate_e.shape[2], (256, 128)))


def kernel(x_prompt, x_sample, rel_table, norm_mix_g, w_in, q_norm_a, k_norm_a, q_norm_b, k_norm_b,
           sink_b, w_proj_a, w_proj_b, w_out, norm_ffn_g, w_router, w_gate_e, w_up_e, w_down_e):
    bp, sp, d = x_prompt.shape
    bs, ss, _ = x_sample.shape
    np_tok = bp * sp
    x = jnp.concatenate([x_prompt.reshape(np_tok, d), x_sample.reshape(bs * ss, d)], axis=0)
    for l in range(norm_mix_g.shape[0]):
        x = _layer(x, np_tok, sp, ss, rel_table, norm_mix_g[l], w_in[l], q_norm_a[l], k_norm_a[l],
                   q_norm_b[l], k_norm_b[l], sink_b[l], w_proj_a[l], w_proj_b[l], w_out[l],
                   norm_ffn_g[l], w_router[l], w_gate_e[l], w_up_e[l], w_down_e[l])
    return x[:np_tok].reshape(bp, sp, d), x[np_tok:].reshape(bs, ss, d)
```

```python
import functools
import math

import jax
import jax.numpy as jnp
from jax import lax
from jax.experimental import pallas as pl
from jax.experimental.pallas import tpu as pltpu

HEAD_DIM = 128
DIL_WINDOWS = (128, 512, 2048)
DIL_RATES = (1, 4, 16)
N_DIL_GROUPS = 3
A_HEADS = 4
B_Q_HEADS = 8
B_KV_HEADS = 2
B_GROUP = B_Q_HEADS // B_KV_HEADS
B_RADIUS = 128
REL_BUCKETS = 32
REL_MAX_DIST = 1024
EC_CAPACITY = 2
NORM_EPS = 1e-6
NEG_INF = -1e30

A_QKV = N_DIL_GROUPS * A_HEADS * HEAD_DIM
A_OUT = A_HEADS * HEAD_DIM
B_Q = B_Q_HEADS * HEAD_DIM
B_KV = B_KV_HEADS * HEAD_DIM

LANES = 128
GROUP_COLS = A_HEADS * HEAD_DIM
VMEM_LIMIT = 60 * 1024 * 1024
ATTN_TQ = 512
COMPACT_CHUNKS = 8
ROW_DMA_UNROLL = 8


def _cparams(sem, vmem=VMEM_LIMIT):
    return pltpu.CompilerParams(dimension_semantics=sem, vmem_limit_bytes=vmem)


def _inproj_kernel(xp_ref, xs_ref, g_ref, w_ref, z_ref, h_scr, *zs_scr, dil, tm, n_p):
    def normalise(x_ref):
        x = x_ref[...]
        ms = jnp.mean(x * x, axis=-1, keepdims=True)
        h_scr[...] = (x * lax.rsqrt(ms + NORM_EPS) * g_ref[...]).astype(jnp.bfloat16)

    first_col = pl.program_id(1) == 0
    is_prompt = pl.program_id(0) < n_p
    pl.when(jnp.logical_and(first_col, is_prompt))(lambda: normalise(xp_ref))
    pl.when(jnp.logical_and(first_col, jnp.logical_not(is_prompt)))(lambda: normalise(xs_ref))

    z = jnp.dot(h_scr[...], w_ref[...], preferred_element_type=jnp.float32)
    if dil == 1:
        z_ref[...] = z.astype(jnp.bfloat16)
    else:
        zs = zs_scr[0]
        tn = z.shape[1]
        rows = tm // dil
        for c in range(tn // LANES):
            zs[c] = z[:, c * LANES:(c + 1) * LANES]
        for r in range(dil):
            for c in range(tn // LANES):
                z_ref[:, r * tn + c * LANES:r * tn + (c + 1) * LANES] = (
                    zs[c, pl.ds(r, rows, stride=dil), :].astype(jnp.bfloat16))


def _inproj(x_p, x_s, g, w_bf16, tm, tn, dil=1):
    d = x_p.shape[1]
    n_p = x_p.shape[0] // tm
    n = x_p.shape[0] + x_s.shape[0]
    cols = w_bf16.shape[1]
    assert dil == 1 or tn == cols
    return pl.pallas_call(
        functools.partial(_inproj_kernel, dil=dil, tm=tm, n_p=n_p),
        grid=(n // tm, cols // tn),
        in_specs=[
            pl.BlockSpec((tm, d), lambda i, j: (jnp.minimum(i, n_p - 1), 0)),
            pl.BlockSpec((tm, d), lambda i, j: (jnp.maximum(i - n_p, 0), 0)),
            pl.BlockSpec((1, d), lambda i, j: (0, 0)),
            pl.BlockSpec((d, tn), lambda i, j: (0, j)),
        ],
        out_specs=pl.BlockSpec((tm // dil, dil * tn), lambda i, j: (i, j)),
        out_shape=jax.ShapeDtypeStruct((n // dil, dil * cols), jnp.bfloat16),
        scratch_shapes=[pltpu.VMEM((tm, d), jnp.bfloat16)]
        + ([pltpu.VMEM((tn // LANES, tm, LANES), jnp.float32)] if dil > 1 else []),
        compiler_params=_cparams(("parallel", "arbitrary")),
        name=f"inproj_d{dil}",
    )(x_p, x_s, g.reshape(1, d), w_bf16)


def _t5_bucket(rel):
    half = REL_BUCKETS // 2
    max_exact = half // 2
    n = jnp.abs(rel)
    base = jnp.where(rel > 0, half, 0)
    nf = jnp.maximum(n, 1).astype(jnp.float32)
    large = max_exact + (jnp.log(nf / max_exact) / math.log(REL_MAX_DIST / max_exact)
                         * (half - max_exact)).astype(jnp.int32)
    large = jnp.minimum(large, half - 1)
    return base + jnp.where(n < max_exact, n, large)


def _bias_tile(table_cols, dil, radius, tq):
    tk = tq + 2 * radius
    n_heads = table_cols.shape[1]
    rel = jnp.arange(-radius, radius + 1)
    vals = table_cols[_t5_bucket(rel * dil)].astype(jnp.float32).T
    period = tq + tk
    w = jnp.full((n_heads, period), NEG_INF, jnp.float32).at[:, :2 * radius + 1].set(vals)
    flat = jnp.tile(w, (1, tq))[:, :tq * (period - 1)]
    return flat.reshape(n_heads, tq, period - 1)[:, :, :tk]


def _head_norm(x, w):
    xf = x.astype(jnp.float32)
    ms = jnp.mean(xf * xf, axis=-1, keepdims=True)
    return xf * lax.rsqrt(ms + NORM_EPS) * w


def _attn_kernel(*refs, tq, radius, shared_kv, with_sink, with_lse, np_rows, len_p, len_s):
    (q_ref, kp_ref, kc_ref, kn_ref, vp_ref, vc_ref, vn_ref, bias_ref, qw_ref, kw_ref) = refs[:10]
    pos = 10
    sink_ref = None
    if with_sink:
        sink_ref = refs[pos]
        pos += 1
    o_ref = refs[pos]
    lse_ref = refs[pos + 1] if with_lse else None

    tk = tq + 2 * radius
    q0 = pl.program_id(1) * tq
    lo_p = (q0 // len_p) * len_p
    lo_s = np_rows + ((q0 - np_rows) // len_s) * len_s
    in_p = q0 < np_rows
    lo = jnp.where(in_p, lo_p, lo_s)
    hi = lo + jnp.where(in_p, len_p, len_s)
    kpos = q0 - radius + lax.broadcasted_iota(jnp.int32, (1, tk), 1)
    valid = (kpos >= lo) & (kpos < hi)

    scale = HEAD_DIM ** -0.5
    k_all = jnp.concatenate([kp_ref[...], kc_ref[...], kn_ref[...]], axis=0)
    v_all = jnp.concatenate([vp_ref[...], vc_ref[...], vn_ref[...]], axis=0)
    qw = qw_ref[...]
    kw = kw_ref[...]
    if shared_kv:
        k_sh = _head_norm(k_all, kw).astype(jnp.bfloat16)

    lane = lax.broadcasted_iota(jnp.int32, (tq, LANES), 1)
    lse_tile = jnp.zeros((tq, LANES), jnp.float32)
    for h in range(A_HEADS):
        cs = slice(h * HEAD_DIM, (h + 1) * HEAD_DIM)
        qh = (_head_norm(q_ref[:, cs], qw) * scale).astype(jnp.bfloat16)
        if shared_kv:
            kh, vh = k_sh, v_all
        else:
            kh = _head_norm(k_all[:, cs], kw).astype(jnp.bfloat16)
            vh = v_all[:, cs]
        s = lax.dot_general(qh, kh, (((1,), (1,)), ((), ())),
                            preferred_element_type=jnp.float32)
        s = jnp.where(valid, s + bias_ref[h], NEG_INF)
        m = jnp.max(s, axis=-1, keepdims=True)
        if with_sink:
            m = jnp.maximum(m, sink_ref[h])
        p = jnp.exp(s - m)
        l = jnp.sum(p, axis=-1, keepdims=True)
        if with_sink:
            l = l + jnp.exp(sink_ref[h] - m)
        o = jnp.dot(p.astype(jnp.bfloat16), vh, preferred_element_type=jnp.float32) / l
        o_ref[:, cs] = o.astype(o_ref.dtype)
        if with_lse:
            lse_tile = jnp.where(lane == h, m + jnp.log(l), lse_tile)
    if with_lse:
        lse_ref[...] = lse_tile


def _banded_attention(zv, *, dil, radius, tq, q_col, k_col, v_col, kv_width, n_par, bias,
                      q_w, k_w, sink, with_lse, np_tok, seq_p, seq_s, out_cols):
    rows = zv.shape[0]
    in_cols = zv.shape[1] // dil
    ntok = rows * dil
    nq = rows // tq
    hb = tq // radius
    n_halo = rows // radius
    shared_kv = kv_width == HEAD_DIM
    q_blk = in_cols // GROUP_COLS
    kv_blk = in_cols // kv_width
    if dil > 1:
        qmap = lambda r, i: (i, r * q_blk + q_col // GROUP_COLS)
        kc = lambda c: (lambda r, i: (i, r * kv_blk + c // kv_width))
        kp = lambda c: (lambda r, i: (jnp.maximum(i * hb - 1, 0), r * kv_blk + c // kv_width))
        kn = lambda c: (lambda r, i: (jnp.minimum((i + 1) * hb, n_halo - 1), r * kv_blk + c // kv_width))
        omap = lambda r, i: (i, r)
    else:
        qmap = lambda r, i: (i, q_col // GROUP_COLS + r)
        kc = lambda c: (lambda r, i: (i, c // kv_width + r))
        kp = lambda c: (lambda r, i: (jnp.maximum(i * hb - 1, 0), c // kv_width + r))
        kn = lambda c: (lambda r, i: (jnp.minimum((i + 1) * hb, n_halo - 1), c // kv_width + r))
        omap = lambda r, i: (i, r)

    tk = tq + 2 * radius
    in_specs = [
        pl.BlockSpec((tq, GROUP_COLS), qmap),
        pl.BlockSpec((radius, kv_width), kp(k_col)),
        pl.BlockSpec((tq, kv_width), kc(k_col)),
        pl.BlockSpec((radius, kv_width), kn(k_col)),
        pl.BlockSpec((radius, kv_width), kp(v_col)),
        pl.BlockSpec((tq, kv_width), kc(v_col)),
        pl.BlockSpec((radius, kv_width), kn(v_col)),
        pl.BlockSpec((A_HEADS, tq, tk), (lambda r, i: (r, 0, 0)) if (dil == 1 and n_par > 1)
                     else (lambda r, i: (0, 0, 0))),
        pl.BlockSpec((1, HEAD_DIM), lambda r, i: (0, 0)),
        pl.BlockSpec((1, HEAD_DIM), lambda r, i: (0, 0)),
    ]
    args = [zv] * 7 + [bias, q_w.reshape(1, HEAD_DIM), k_w.reshape(1, HEAD_DIM)]
    if sink is not None:
        in_specs.append(pl.BlockSpec((A_HEADS, 1, 1), lambda r, i: (r, 0, 0)))
        args.append(sink.reshape(-1, 1, 1).astype(jnp.float32))
    out_specs = [pl.BlockSpec((tq, GROUP_COLS), omap)]
    out_shape = [jax.ShapeDtypeStruct((rows, n_par * GROUP_COLS), jnp.bfloat16)]
    if with_lse:
        out_specs.append(pl.BlockSpec((tq, LANES), omap))
        out_shape.append(jax.ShapeDtypeStruct((rows, n_par * LANES), jnp.float32))
    kern = functools.partial(
        _attn_kernel, tq=tq, radius=radius, shared_kv=shared_kv, with_sink=sink is not None,
        with_lse=with_lse, np_rows=np_tok // dil, len_p=seq_p // dil, len_s=seq_s // dil)
    outs = pl.pallas_call(
        kern, grid=(n_par, nq), in_specs=in_specs, out_specs=out_specs, out_shape=out_shape,
        compiler_params=_cparams(("parallel", "arbitrary")),
        name=f"band_attn_d{dil}_r{radius}",
    )(*args)
    o = outs[0].reshape(ntok, out_cols)
    if with_lse:
        return o, outs[1].reshape(ntok, LANES)
    return o


def _merge_kernel(xp_ref, xs_ref, oa0_ref, oa1_ref, oa2_ref, l0_ref, l1_ref, l2_ref, ob_ref, ga_ref,
                  gb_ref, wpa_ref, wpb_ref, wo_ref, g2_ref, wr_ref, x1p_ref, x1s_ref, h2p_ref,
                  h2s_ref, aff_ref, *, tm, n_p):
    is_prompt = pl.program_id(0) < n_p
    l0, l1, l2 = l0_ref[...], l1_ref[...], l2_ref[...]
    mx = jnp.maximum(jnp.maximum(l0, l1), l2)
    e0, e1, e2 = jnp.exp(l0 - mx), jnp.exp(l1 - mx), jnp.exp(l2 - mx)
    den = e0 + e1 + e2
    w0, w1, w2 = e0 / den, e1 / den, e2 / den
    parts = []
    for h in range(A_HEADS):
        cs = slice(h * HEAD_DIM, (h + 1) * HEAD_DIM)
        parts.append(w0[:, h:h + 1] * oa0_ref[:, cs].astype(jnp.float32)
                     + w1[:, h:h + 1] * oa1_ref[:, cs].astype(jnp.float32)
                     + w2[:, h:h + 1] * oa2_ref[:, cs].astype(jnp.float32))
    o_a = jnp.concatenate(parts, axis=1).astype(jnp.bfloat16)
    pa = jnp.dot(o_a, wpa_ref[...], preferred_element_type=jnp.float32)
    pb = jnp.dot(ob_ref[...], wpb_ref[...], preferred_element_type=jnp.float32)
    merged = (jax.nn.sigmoid(ga_ref[...].astype(jnp.float32)) * pa
              + jax.nn.sigmoid(gb_ref[...].astype(jnp.float32)) * pb)
    delta = jnp.dot(merged.astype(jnp.bfloat16), wo_ref[...], preferred_element_type=jnp.float32)

    def finish(x_ref, x1_ref, h2_ref):
        x1 = x_ref[...] + delta
        x1_ref[...] = x1
        ms = jnp.mean(x1 * x1, axis=-1, keepdims=True)
        h2 = x1 * lax.rsqrt(ms + NORM_EPS) * g2_ref[...]
        h2_ref[...] = h2
        logits = lax.dot_general(wr_ref[...], h2.astype(jnp.bfloat16), (((1,), (1,)), ((), ())),
                                 preferred_element_type=jnp.float32)
        mx2 = jnp.max(logits, axis=0, keepdims=True)
        ex = jnp.exp(logits - mx2)
        aff = ex / jnp.sum(ex, axis=0, keepdims=True)
        for j in range(tm // LANES):
            aff_ref[j] = aff[:, j * LANES:(j + 1) * LANES]

    pl.when(is_prompt)(lambda: finish(xp_ref, x1p_ref, h2p_ref))
    pl.when(jnp.logical_not(is_prompt))(lambda: finish(xs_ref, x1s_ref, h2s_ref))


def _merge(x_p, x_s, oas, lses, ob, z, wpa, wpb, wo, g2, wr_t, tm):
    d = x_p.shape[1]
    n_p = x_p.shape[0] // tm
    n = x_p.shape[0] + x_s.shape[0]
    n_exp = wr_t.shape[0]
    ga_blk = 0
    row = lambda i: (i, 0)
    const = lambda i: (0, 0)
    row_p = lambda i: (jnp.minimum(i, n_p - 1), 0)
    row_s = lambda i: (jnp.maximum(i - n_p, 0), 0)
    in_specs = [
        pl.BlockSpec((tm, d), row_p),
        pl.BlockSpec((tm, d), row_s),
        pl.BlockSpec((tm, A_OUT), row), pl.BlockSpec((tm, A_OUT), row), pl.BlockSpec((tm, A_OUT), row),
        pl.BlockSpec((tm, LANES), row), pl.BlockSpec((tm, LANES), row), pl.BlockSpec((tm, LANES), row),
        pl.BlockSpec((tm, B_Q), row),
        pl.BlockSpec((tm, d), lambda i: (i, ga_blk)),
        pl.BlockSpec((tm, d), lambda i: (i, ga_blk + 1)),
        pl.BlockSpec((A_OUT, d), const),
        pl.BlockSpec((B_Q, d), const),
        pl.BlockSpec((d, d), const),
        pl.BlockSpec((1, d), const),
        pl.BlockSpec((n_exp, d), const),
    ]
    out_specs = [
        pl.BlockSpec((tm, d), row_p),
        pl.BlockSpec((tm, d), row_s),
        pl.BlockSpec((tm, d), row_p),
        pl.BlockSpec((tm, d), row_s),
        pl.BlockSpec((tm // LANES, n_exp, LANES), lambda i: (i, 0, 0)),
    ]
    out_shape = [
        jax.ShapeDtypeStruct(x_p.shape, jnp.float32),
        jax.ShapeDtypeStruct(x_s.shape, jnp.float32),
        jax.ShapeDtypeStruct(x_p.shape, jnp.float32),
        jax.ShapeDtypeStruct(x_s.shape, jnp.float32),
        jax.ShapeDtypeStruct((n // LANES, n_exp, LANES), jnp.float32),
    ]
    return pl.pallas_call(
        functools.partial(_merge_kernel, tm=tm, n_p=n_p),
        grid=(n // tm,), in_specs=in_specs, out_specs=out_specs, out_shape=out_shape,
        compiler_params=_cparams(("arbitrary",)),
        name="merge_proj_router",
    )(x_p, x_s, oas[0], oas[1], oas[2], lses[0], lses[1], lses[2], ob, z, z, wpa, wpb, wo,
      g2.reshape(1, d), wr_t)


def _prefix_counts(flag_f32, tri, tot_scr, off_scr, nc):
    n_exp = flag_f32.shape[1]
    incl = jnp.dot(flag_f32.astype(jnp.bfloat16).reshape(nc * n_exp, LANES), tri,
                   preferred_element_type=jnp.float32).reshape(nc, n_exp, LANES)
    tot_scr[...] = jnp.broadcast_to(incl[:, :, LANES - 1:LANES], (nc, n_exp, LANES))

    def body(c, run):
        off_scr[c] = run
        return run + tot_scr[c]

    lax.fori_loop(0, nc, body, jnp.zeros((n_exp, LANES), jnp.float32))
    return off_scr[...] + incl - flag_f32


def _route_kernel(aff_ref, pos_ref, off_ref, tot_scr, off_scr, *, cap, nc):
    n_exp = aff_ref.shape[1]
    capf = jnp.float32(cap)

    def count(mask):
        c = jnp.sum(jnp.where(mask, 1.0, 0.0), axis=0)
        return jnp.sum(c, axis=-1, keepdims=True)

    def bit_body(k, t):
        cand = t | jnp.left_shift(jnp.int32(1), 30 - k)
        bits = pltpu.bitcast(aff_ref[...], jnp.int32)
        return jnp.where(count(bits >= cand[None]) >= capf, cand, t)

    t = lax.fori_loop(0, 31, bit_body, jnp.zeros((n_exp, 1), jnp.int32))
    bits = pltpu.bitcast(aff_ref[...], jnp.int32)
    gt = bits > t[None]
    eq = bits == t[None]
    need = capf - count(gt)
    rows = lax.broadcasted_iota(jnp.int32, (LANES, LANES), 0)
    cols = lax.broadcasted_iota(jnp.int32, (LANES, LANES), 1)
    tri = jnp.where(rows <= cols, 1.0, 0.0).astype(jnp.bfloat16)
    eq_f = jnp.where(eq, 1.0, 0.0)
    tie_rank = _prefix_counts(eq_f, tri, tot_scr, off_scr, nc)
    sel = gt | (eq & (tie_rank < need[None]))
    sel_f = jnp.where(sel, 1.0, 0.0)
    slot = _prefix_counts(sel_f, tri, tot_scr, off_scr, nc)
    pos_ref[...] = jnp.where(sel, slot, -1.0).astype(jnp.int32)
    off_ref[...] = off_scr[...].astype(jnp.int32)


def _route(aff, cap):
    nc, n_exp, _ = aff.shape
    full = pl.BlockSpec((nc, n_exp, LANES), lambda: (0, 0, 0))
    return pl.pallas_call(
        functools.partial(_route_kernel, cap=cap, nc=nc),
        in_specs=[full], out_specs=[full, full],
        out_shape=[jax.ShapeDtypeStruct((nc, n_exp, LANES), jnp.int32)] * 2,
        scratch_shapes=[pltpu.VMEM((nc, n_exp, LANES), jnp.float32)] * 2,
        compiler_params=pltpu.CompilerParams(vmem_limit_bytes=VMEM_LIMIT),
        name="route_select",
    )(aff)


def _split3(x):
    a = x.astype(jnp.bfloat16).astype(jnp.float32)
    r = x - a
    b = r.astype(jnp.bfloat16).astype(jnp.float32)
    c = r - b
    return a, b, c


def _compact_kernel(off_smem, pos_ref, aff_ref, idx_ref, gate_ref, acc_scr, *, nc, n_blk, tok_base):
    e = pl.program_id(0)
    stride = nc + 1
    sub = lax.broadcasted_iota(jnp.int32, (LANES, LANES), 0)
    sub16 = lax.broadcasted_iota(jnp.int32, (16, LANES), 0)
    lane = lax.broadcasted_iota(jnp.int32, (1, LANES), 1)

    n_grp = nc // COMPACT_CHUNKS

    def chunk_off(c):
        return off_smem[e * stride + jnp.minimum(c, nc)]

    def block_body(sb, g_first):
        lo_slot = sb * LANES

        def skip_cond(g):
            return jnp.logical_and(g < n_grp, chunk_off((g + 1) * COMPACT_CHUNKS) <= lo_slot)

        g_first = lax.while_loop(skip_cond, lambda g: g + 1, g_first)
        acc_scr[...] = jnp.zeros_like(acc_scr)

        def take_cond(g):
            return jnp.logical_and(g < n_grp, chunk_off(g * COMPACT_CHUNKS) < lo_slot + LANES)

        def take(g):
            total = jnp.zeros((16, LANES), jnp.float32)
            for k in range(COMPACT_CHUNKS):
                c = g * COMPACT_CHUNKS + k
                rel = pos_ref[c, pl.ds(e, 1), :] - lo_slot
                onehot = jnp.where(sub == rel, 1.0, 0.0).astype(jnp.bfloat16)
                tok = tok_base + c * LANES + lane
                g1, g2, g3 = _split3(aff_ref[c, pl.ds(e, 1), :])
                lhs = jnp.where(sub16 == 0, (tok >> 8).astype(jnp.float32),
                      jnp.where(sub16 == 1, (tok & 255).astype(jnp.float32),
                      jnp.where(sub16 == 2, g1,
                      jnp.where(sub16 == 3, g2,
                      jnp.where(sub16 == 4, g3, 0.0))))).astype(jnp.bfloat16)
                total = total + lax.dot_general(lhs, onehot, (((1,), (1,)), ((), ())),
                                                preferred_element_type=jnp.float32)
            acc_scr[...] += total
            return g + 1

        lax.while_loop(take_cond, take, g_first)
        acc = acc_scr[...]
        idx_ref[0, pl.ds(sb, 1), :] = (acc[0:1] * 256.0 + acc[1:2]).astype(jnp.int32)
        gate_ref[0, pl.ds(sb, 1), :] = (acc[2:3] + acc[3:4]) + acc[4:5]
        return g_first

    lax.fori_loop(0, n_blk, block_body, jnp.int32(0))


def _compact(offs_flat, pos, aff, cap, tok_base):
    nc, n_exp, _ = pos.shape
    n_blk = cap // LANES
    full = pl.BlockSpec((nc, n_exp, LANES), lambda e, off: (0, 0, 0))
    out = pl.BlockSpec((1, n_blk, LANES), lambda e, off: (e, 0, 0))
    return pl.pallas_call(
        functools.partial(_compact_kernel, nc=nc, n_blk=n_blk, tok_base=tok_base),
        grid_spec=pltpu.PrefetchScalarGridSpec(
            num_scalar_prefetch=1, grid=(n_exp,), in_specs=[full, full], out_specs=[out, out],
            scratch_shapes=[pltpu.VMEM((16, LANES), jnp.float32)]),
        out_shape=[jax.ShapeDtypeStruct((n_exp, n_blk, LANES), jnp.int32),
                   jax.ShapeDtypeStruct((n_exp, n_blk, LANES), jnp.float32)],
        compiler_params=_cparams(("arbitrary",)),
        name="route_compact",
    )(offs_flat, pos, aff)


def _route_set(aff, cap, tok_base):
    nc, n_exp, _ = aff.shape
    pos, off = _route(aff, cap)
    offs = jnp.concatenate([off[:, :, 0].T, jnp.full((n_exp, 1), cap, jnp.int32)], axis=1)
    idx, gate = _compact(offs.reshape(-1), pos, aff, cap, tok_base)
    return idx.reshape(n_exp, cap), gate.reshape(n_exp, cap)


MOE_STEPS_SCATTER = 4
MOE_STEPS_GAP = 2


def _moe_kernel(idx_prev, idx_cur, idx_nxt, gate_ref, wg_hbm, wu_hbm, wd_hbm, h2_hbm, x1_hbm,
                out_hbm, xe32, xe16, acc, orow, wg_buf, wu_buf, wd_buf, sems, wsems,
                *, tm, tf, nt, n_tiles, n_f):
    del x1_hbm
    n = pl.program_id(0)
    sem_x, sem_g, sem_s = sems.at[0], sems.at[1], sems.at[2]
    last_step = n_tiles * n_f - 1
    steps_b = n_f - MOE_STEPS_SCATTER - MOE_STEPS_GAP
    rows_a = tm // MOE_STEPS_SCATTER
    rows_gap = tm // 2 // MOE_STEPS_GAP
    rows_bx = tm // 2 // steps_b
    rows_bg = tm // steps_b

    def weight_copies(step, slot):
        step = jnp.minimum(step, last_step)
        e = step // (nt * n_f)
        col = pl.multiple_of((step % n_f) * tf, tf)
        return (
            pltpu.make_async_copy(wg_hbm.at[e, :, pl.ds(col, tf)], wg_buf.at[slot], wsems.at[0, slot]),
            pltpu.make_async_copy(wu_hbm.at[e, :, pl.ds(col, tf)], wu_buf.at[slot], wsems.at[1, slot]),
            pltpu.make_async_copy(wd_hbm.at[e, pl.ds(col, tf), :], wd_buf.at[slot], wsems.at[2, slot]),
        )

    def xe_row(idx_smem, s):
        return pltpu.make_async_copy(h2_hbm.at[pl.ds(idx_smem[0, 0, s], 1)], xe32.at[pl.ds(s, 1)], sem_x)

    def residual_row(s):
        return pltpu.make_async_copy(out_hbm.at[pl.ds(idx_cur[0, 0, s], 1)], orow.at[pl.ds(s, 1)], sem_g)

    def scatter_row(idx_smem, s):
        return pltpu.make_async_copy(orow.at[pl.ds(s, 1)], out_hbm.at[pl.ds(idx_smem[0, 0, s], 1)], sem_s)

    def wait_xe():
        pltpu.make_async_copy(h2_hbm.at[pl.ds(0, tm)], xe32, sem_x).wait()

    def wait_scatter():
        pltpu.make_async_copy(orow, out_hbm.at[pl.ds(0, tm)], sem_s).wait()

    def ffn_step(f, slot):
        for c in weight_copies(n * n_f + f, slot):
            c.wait()
        for c in weight_copies(n * n_f + f + 1, 1 - slot):
            c.start()
        x = xe16[...]
        gp = jnp.dot(x, wg_buf[slot].astype(jnp.bfloat16), preferred_element_type=jnp.float32)
        up = jnp.dot(x, wu_buf[slot].astype(jnp.bfloat16), preferred_element_type=jnp.float32)
        hid = (gp * jax.nn.sigmoid(gp) * up).astype(jnp.bfloat16)
        acc[...] += jnp.dot(hid, wd_buf[slot].astype(jnp.bfloat16),
                            preferred_element_type=jnp.float32)

    def run_steps(f_lo, count, row_work):
        def body(p, _):
            for slot in (0, 1):
                ffn_step(f_lo + 2 * p + slot, slot)
                row_work(2 * p + slot)
            return 0
        lax.fori_loop(0, count // 2, body, 0)

    @pl.when(n == 0)
    def _():
        def body(s, _):
            xe_row(idx_cur, s).start()
            return 0
        lax.fori_loop(0, tm, body, 0, unroll=ROW_DMA_UNROLL)
        for c in weight_copies(0, 0):
            c.start()

    wait_xe()
    xe16[...] = xe32[...].astype(jnp.bfloat16)
    acc[...] = jnp.zeros_like(acc)

    def scatter_prev(k):
        for r in range(rows_a):
            scatter_row(idx_prev, k * rows_a + r).start()

    @pl.when(n > 0)
    def _():
        run_steps(0, MOE_STEPS_SCATTER, scatter_prev)

    @pl.when(n == 0)
    def _():
        run_steps(0, MOE_STEPS_SCATTER, lambda k: None)

    def gather_next_first_half(k):
        for r in range(rows_gap):
            xe_row(idx_nxt, k * rows_gap + r).start()

    run_steps(MOE_STEPS_SCATTER, MOE_STEPS_GAP, gather_next_first_half)

    @pl.when(n > 0)
    def _():
        wait_scatter()

    def gather_residual_and_next(k):
        for r in range(rows_bg):
            residual_row(k * rows_bg + r).start()
        for r in range(rows_bx):
            xe_row(idx_nxt, tm // 2 + k * rows_bx + r).start()

    run_steps(MOE_STEPS_SCATTER + MOE_STEPS_GAP, steps_b, gather_residual_and_next)

    pltpu.make_async_copy(out_hbm.at[pl.ds(0, tm)], orow, sem_g).wait()
    g_t = gate_ref[0].T
    for j in range(tm // LANES):
        rs = slice(j * LANES, (j + 1) * LANES)
        orow[rs, :] = orow[rs, :] + acc[rs, :] * g_t[:, j:j + 1]

    @pl.when(n == n_tiles - 1)
    def _():
        def body(s, _):
            scatter_row(idx_cur, s).start()
            return 0
        lax.fori_loop(0, tm, body, 0, unroll=ROW_DMA_UNROLL)
        wait_scatter()
        wait_xe()
        for c in weight_copies(last_step, 0):
            c.wait()


def _moe(idx, gate, h2, x1, wg, wu, wd, tm, tf):
    n_exp, slots = idx.shape
    ntok, d = x1.shape
    ff = wg.shape[2]
    nt = slots // tm
    n_f = ff // tf
    n_tiles = n_exp * nt
    steps_b = n_f - MOE_STEPS_SCATTER - MOE_STEPS_GAP
    assert n_f % 2 == 0 and steps_b > 0 and steps_b % 2 == 0 and (tm // 2) % steps_b == 0
    idx3 = idx.reshape(n_tiles, 1, tm)
    gate3 = gate.reshape(n_exp, slots // LANES, LANES)
    smem = pltpu.MemorySpace.SMEM
    hbm = pl.BlockSpec(memory_space=pl.ANY)
    in_specs = [
        pl.BlockSpec((1, 1, tm), lambda n: (jnp.maximum(n - 1, 0), 0, 0), memory_space=smem),
        pl.BlockSpec((1, 1, tm), lambda n: (n, 0, 0), memory_space=smem),
        pl.BlockSpec((1, 1, tm), lambda n: (jnp.minimum(n + 1, n_tiles - 1), 0, 0), memory_space=smem),
        pl.BlockSpec((1, tm // LANES, LANES), lambda n: (n // nt, n % nt, 0)),
        hbm, hbm, hbm, hbm, hbm,
    ]
    return pl.pallas_call(
        functools.partial(_moe_kernel, tm=tm, tf=tf, nt=nt, n_tiles=n_tiles, n_f=n_f),
        grid=(n_tiles,), in_specs=in_specs, out_specs=hbm,
        out_shape=jax.ShapeDtypeStruct((ntok, d), jnp.float32),
        scratch_shapes=[
            pltpu.VMEM((tm, d), jnp.float32),
            pltpu.VMEM((tm, d), jnp.bfloat16),
            pltpu.VMEM((tm, d), jnp.float32),
            pltpu.VMEM((tm, d), jnp.float32),
            pltpu.VMEM((2, d, tf), jnp.float32),
            pltpu.VMEM((2, d, tf), jnp.float32),
            pltpu.VMEM((2, tf, d), jnp.float32),
            pltpu.SemaphoreType.DMA((3,)),
            pltpu.SemaphoreType.DMA((3, 2)),
        ],
        input_output_aliases={8: 0},
        compiler_params=_cparams(("arbitrary",)),
        name="expert_ffn",
    )(idx3, idx3, idx3, gate3, wg, wu, wd, h2, x1)


def _pick(n, candidates):
    for c in candidates:
        if n % c == 0:
            return c
    raise ValueError(f"no tile among {candidates} divides {n}")


def _layer(x_p, x_s, seq_p, seq_s, rel_table, norm_mix_g, w_in, q_norm_a, k_norm_a, q_norm_b,
           k_norm_b, sink_b, w_proj_a, w_proj_b, w_out, norm_ffn_g, w_router, w_gate_e, w_up_e,
           w_down_e):
    np_tok, d = x_p.shape
    ns_tok = x_s.shape[0]
    ntok = np_tok + ns_tok
    bf = jnp.bfloat16
    tm_in = _pick(math.gcd(np_tok, ns_tok), (1024, 512, 256))

    def group_cols(g):
        return [w_in[:, s * A_QKV + g * GROUP_COLS: s * A_QKV + (g + 1) * GROUP_COLS] for s in range(3)]

    w_nat = jnp.concatenate([w_in[:, 3 * A_QKV + B_Q + 2 * B_KV:]] + group_cols(0)
                            + [w_in[:, 3 * A_QKV:3 * A_QKV + B_Q + 2 * B_KV]], axis=1).astype(bf)
    z = _inproj(x_p, x_s, norm_mix_g, w_nat, tm_in, _pick(w_nat.shape[1], (1024, 512)))
    a0 = 2 * d
    b0 = a0 + 3 * GROUP_COLS

    oas, lses = [], []
    for g in range(N_DIL_GROUPS):
        dil = DIL_RATES[g]
        radius = (DIL_WINDOWS[g] // 2) // dil
        tq = min(ATTN_TQ, seq_p // dil, seq_s // dil)
        bias = _bias_tile(rel_table[:, g * A_HEADS:(g + 1) * A_HEADS], dil, radius, tq)
        if dil == 1:
            zv, c0 = z, a0
        else:
            w_g = jnp.concatenate(group_cols(g), axis=1).astype(bf)
            zv, c0 = _inproj(x_p, x_s, norm_mix_g, w_g, tm_in, w_g.shape[1], dil=dil), 0
        o, lse = _banded_attention(
            zv, dil=dil, radius=radius, tq=tq, q_col=c0, k_col=c0 + GROUP_COLS,
            v_col=c0 + 2 * GROUP_COLS, kv_width=GROUP_COLS, n_par=dil, bias=bias,
            q_w=q_norm_a, k_w=k_norm_a, sink=None, with_lse=True, np_tok=np_tok, seq_p=seq_p,
            seq_s=seq_s, out_cols=A_OUT)
        oas.append(o)
        lses.append(lse)
    tq_b = min(ATTN_TQ, seq_p, seq_s)
    bias_b = _bias_tile(rel_table[:, N_DIL_GROUPS * A_HEADS:], 1, B_RADIUS, tq_b)
    ob = _banded_attention(
        z, dil=1, radius=B_RADIUS, tq=tq_b, q_col=b0, k_col=b0 + B_Q,
        v_col=b0 + B_Q + B_KV, kv_width=HEAD_DIM, n_par=B_KV_HEADS, bias=bias_b,
        q_w=q_norm_b, k_w=k_norm_b, sink=sink_b, with_lse=False, np_tok=np_tok, seq_p=seq_p,
        seq_s=seq_s, out_cols=B_Q)

    x1_p, x1_s, h2_p, h2_s, aff = _merge(
        x_p, x_s, oas, lses, ob, z, w_proj_a.astype(bf), w_proj_b.astype(bf), w_out.astype(bf),
        norm_ffn_g, w_router.T.astype(bf), _pick(math.gcd(np_tok, ns_tok), (256,)))

    n_exp = w_router.shape[1]
    tf = _pick(w_gate_e.shape[2], (256, 128))
    outs = []
    for x1, h2, aff_set in ((x1_p, h2_p, aff[:np_tok // LANES]), (x1_s, h2_s, aff[np_tok // LANES:])):
        cap = max(1, EC_CAPACITY * x1.shape[0] // n_exp)
        idx, gate = _route_set(aff_set, cap, 0)
        outs.append(_moe(idx, gate, h2, x1, w_gate_e, w_up_e, w_down_e, _pick(cap, (1024,)), tf))
    return outs


def kernel(x_prompt, x_sample, rel_table, norm_mix_g, w_in, q_norm_a, k_norm_a, q_norm_b, k_norm_b,
           sink_b, w_proj_a, w_proj_b, w_out, norm_ffn_g, w_router, w_gate_e, w_up_e, w_down_e):
    bp, sp, d = x_prompt.shape
    bs, ss, _ = x_sample.shape
    x_p = x_prompt.reshape(bp * sp, d)
    x_s = x_sample.reshape(bs * ss, d)
    for l in range(norm_mix_g.shape[0]):
        x_p, x_s = _layer(x_p, x_s, sp, ss, rel_table, norm_mix_g[l], w_in[l], q_norm_a[l],
                          k_norm_a[l], q_norm_b[l], k_norm_b[l], sink_b[l], w_proj_a[l],
                          w_proj_b[l], w_out[l], norm_ffn_g[l], w_router[l], w_gate_e[l],
                          w_up_e[l], w_down_e[l])
    return x_p.reshape(bp, sp, d), x_s.reshape(bs, ss, d)
```

```python
import functools
import math

import jax
import jax.numpy as jnp
from jax import lax
from jax.experimental import pallas as pl
from jax.experimental.pallas import tpu as pltpu

HEAD_DIM = 128
DIL_WINDOWS = (128, 512, 2048)
DIL_RATES = (1, 4, 16)
N_DIL_GROUPS = 3
A_HEADS = 4
B_Q_HEADS = 8
B_KV_HEADS = 2
B_GROUP = B_Q_HEADS // B_KV_HEADS
B_RADIUS = 128
REL_BUCKETS = 32
REL_MAX_DIST = 1024
EC_CAPACITY = 2
NORM_EPS = 1e-6
NEG_INF = -1e30

A_QKV = N_DIL_GROUPS * A_HEADS * HEAD_DIM
A_OUT = A_HEADS * HEAD_DIM
B_Q = B_Q_HEADS * HEAD_DIM
B_KV = B_KV_HEADS * HEAD_DIM

LANES = 128
GROUP_COLS = A_HEADS * HEAD_DIM
VMEM_LIMIT = 60 * 1024 * 1024
QKV_COLS = 3 * GROUP_COLS
ATTN_TQ = 512
ATTN_SUB = 128
ATTN_SUB_B = 128
ATTN_RES_PER_STEP = 4
COMPACT_CHUNKS = 8
ROW_DMA_UNROLL = 8


def _cparams(sem, vmem=VMEM_LIMIT):
    return pltpu.CompilerParams(dimension_semantics=sem, vmem_limit_bytes=vmem)


def _inproj_kernel(xp_ref, xs_ref, g_ref, w_ref, z_ref, h_scr, *zs_scr, dil, tm, n_p):
    def normalise(x_ref):
        x = x_ref[...]
        ms = jnp.mean(x * x, axis=-1, keepdims=True)
        h_scr[...] = (x * lax.rsqrt(ms + NORM_EPS) * g_ref[...]).astype(jnp.bfloat16)

    first_col = pl.program_id(1) == 0
    is_prompt = pl.program_id(0) < n_p
    pl.when(jnp.logical_and(first_col, is_prompt))(lambda: normalise(xp_ref))
    pl.when(jnp.logical_and(first_col, jnp.logical_not(is_prompt)))(lambda: normalise(xs_ref))

    z = jnp.dot(h_scr[...], w_ref[...], preferred_element_type=jnp.float32)
    if dil == 1:
        z_ref[...] = z.astype(jnp.bfloat16)
    else:
        zs = zs_scr[0]
        tn = z.shape[1]
        rows = tm // dil
        for c in range(tn // LANES):
            zs[c] = z[:, c * LANES:(c + 1) * LANES]
        for r in range(dil):
            for c in range(tn // LANES):
                z_ref[:, r * tn + c * LANES:r * tn + (c + 1) * LANES] = (
                    zs[c, pl.ds(r, rows, stride=dil), :].astype(jnp.bfloat16))


def _inproj(x_p, x_s, g, w_bf16, tm, tn, dil=1):
    d = x_p.shape[1]
    n_p = x_p.shape[0] // tm
    n = x_p.shape[0] + x_s.shape[0]
    cols = w_bf16.shape[1]
    assert dil == 1 or tn == cols
    return pl.pallas_call(
        functools.partial(_inproj_kernel, dil=dil, tm=tm, n_p=n_p),
        grid=(n // tm, cols // tn),
        in_specs=[
            pl.BlockSpec((tm, d), lambda i, j: (jnp.minimum(i, n_p - 1), 0)),
            pl.BlockSpec((tm, d), lambda i, j: (jnp.maximum(i - n_p, 0), 0)),
            pl.BlockSpec((1, d), lambda i, j: (0, 0)),
            pl.BlockSpec((d, tn), lambda i, j: (0, j)),
        ],
        out_specs=pl.BlockSpec((tm // dil, dil * tn), lambda i, j: (i, j)),
        out_shape=jax.ShapeDtypeStruct((n // dil, dil * cols), jnp.bfloat16),
        scratch_shapes=[pltpu.VMEM((tm, d), jnp.bfloat16)]
        + ([pltpu.VMEM((tn // LANES, tm, LANES), jnp.float32)] if dil > 1 else []),
        compiler_params=_cparams(("parallel", "arbitrary")),
        name=f"inproj_d{dil}",
    )(x_p, x_s, g.reshape(1, d), w_bf16)


def _t5_bucket(rel):
    half = REL_BUCKETS // 2
    max_exact = half // 2
    n = jnp.abs(rel)
    base = jnp.where(rel > 0, half, 0)
    nf = jnp.maximum(n, 1).astype(jnp.float32)
    large = max_exact + (jnp.log(nf / max_exact) / math.log(REL_MAX_DIST / max_exact)
                         * (half - max_exact)).astype(jnp.int32)
    large = jnp.minimum(large, half - 1)
    return base + jnp.where(n < max_exact, n, large)


def _bias_tile(table_cols, dil, radius, tq):
    tk = tq + 2 * radius
    n_heads = table_cols.shape[1]
    rel = jnp.arange(-radius, radius + 1)
    vals = table_cols[_t5_bucket(rel * dil)].astype(jnp.float32).T
    period = tq + tk
    w = jnp.full((n_heads, period), NEG_INF, jnp.float32).at[:, :2 * radius + 1].set(vals)
    flat = jnp.tile(w, (1, tq))[:, :tq * (period - 1)]
    return flat.reshape(n_heads, tq, period - 1)[:, :, :tk]


def _head_norm(x, w):
    xf = x.astype(jnp.float32)
    ms = jnp.mean(xf * xf, axis=-1, keepdims=True)
    return xf * lax.rsqrt(ms + NORM_EPS) * w


def _attn_kernel(*refs, tq, sub, radius, heads, res_per_step, with_sink, with_lse, np_rows,
                 len_p, len_s):
    cur_ref, prev_ref, next_ref, bias_ref, qw_ref, kw_ref = refs[:6]
    pos = 6
    sink_ref = None
    if with_sink:
        sink_ref = refs[pos]
        pos += 1
    o_ref = refs[pos]
    lse_ref = refs[pos + 1] if with_lse else None

    win = sub + 2 * radius
    n_heads = len(heads)
    q0 = pl.program_id(1) * tq
    lo_p = (q0 // len_p) * len_p
    lo_s = np_rows + ((q0 - np_rows) // len_s) * len_s
    in_p = q0 < np_rows
    lo = jnp.where(in_p, lo_p, lo_s)
    hi = lo + jnp.where(in_p, len_p, len_s)
    key_iota = lax.broadcasted_iota(jnp.int32, (1, win), 1)
    lane = lax.broadcasted_iota(jnp.int32, (sub, LANES), 1)
    scale = HEAD_DIM ** -0.5
    qw = qw_ref[...]
    kw = kw_ref[...]

    def window(col):
        cs = slice(col, col + HEAD_DIM)
        return jnp.concatenate([prev_ref[:, cs], cur_ref[:, cs], next_ref[:, cs]], axis=0)

    kv_groups = {}
    for h, (_, k_off, v_off) in enumerate(heads):
        kv_groups.setdefault((k_off, v_off), []).append(h)

    def stack(pieces):
        return pieces[0] if len(pieces) == 1 else jnp.concatenate(pieces, axis=0)

    for rr in range(res_per_step):
        base = rr * QKV_COLS
        lse_tiles = [jnp.zeros((sub, LANES), jnp.float32) for _ in range(tq // sub)]
        for (k_off, v_off), hs in kv_groups.items():
            kh = _head_norm(window(base + k_off), kw).astype(jnp.bfloat16)
            vh = window(base + v_off)
            qs = [(_head_norm(cur_ref[:, base + heads[h][0]:base + heads[h][0] + HEAD_DIM], qw)
                   * scale).astype(jnp.bfloat16) for h in hs]
            bias = stack([bias_ref[h] for h in hs])
            if with_sink:
                sink = stack([sink_ref[h] for h in hs])
                is_sink = sink > NEG_INF
            for a in range(tq // sub):
                kpos = q0 + a * sub - radius + key_iota
                valid = (kpos >= lo) & (kpos < hi)
                s = lax.dot_general(stack([q[a * sub:(a + 1) * sub] for q in qs]),
                                    kh[a * sub:a * sub + win], (((1,), (1,)), ((), ())),
                                    preferred_element_type=jnp.float32)
                s = jnp.where(valid, s + bias, NEG_INF)
                if with_sink:
                    s = jnp.maximum(s, sink)
                m = jnp.max(s, axis=-1, keepdims=True)
                p = jnp.exp(s - m)
                l = jnp.sum(p, axis=-1, keepdims=True)
                if with_sink:
                    p = jnp.where(is_sink, 0.0, p)
                o = jnp.dot(p.astype(jnp.bfloat16), vh[a * sub:a * sub + win],
                            preferred_element_type=jnp.float32) / l
                lse = m + jnp.log(l)
                for gi, h in enumerate(hs):
                    oc = (rr * n_heads + h) * HEAD_DIM
                    o_ref[a * sub:(a + 1) * sub, oc:oc + HEAD_DIM] = (
                        o[gi * sub:(gi + 1) * sub].astype(o_ref.dtype))
                    if with_lse:
                        lse_tiles[a] = jnp.where(lane == h, lse[gi * sub:(gi + 1) * sub],
                                                 lse_tiles[a])
        if with_lse:
            for a in range(tq // sub):
                lse_ref[a * sub:(a + 1) * sub, rr * LANES:(rr + 1) * LANES] = lse_tiles[a]


def _banded_attention(zv, *, dil, col0, radius, tq, heads, res_per_step, bias, q_w, k_w, sink,
                      with_lse, np_tok, seq_p, seq_s):
    rows = zv.shape[0]
    ntok = rows * dil
    n_heads = len(heads)
    assert col0 % QKV_COLS == 0 and dil % res_per_step == 0
    assert dil == 1 or zv.shape[1] == dil * QKV_COLS
    blk0 = col0 // QKV_COLS
    nq = rows // tq
    hb = tq // radius
    n_halo = rows // radius
    sub = bias.shape[1]
    wblk = res_per_step * QKV_COLS
    const2 = lambda r, i: (0, 0)
    const3 = lambda r, i: (0, 0, 0)
    in_specs = [
        pl.BlockSpec((tq, wblk), lambda r, i: (i, blk0 + r)),
        pl.BlockSpec((radius, wblk), lambda r, i: (jnp.maximum(i * hb - 1, 0), blk0 + r)),
        pl.BlockSpec((radius, wblk), lambda r, i: (jnp.minimum((i + 1) * hb, n_halo - 1), blk0 + r)),
        pl.BlockSpec((n_heads, sub, sub + 2 * radius), const3),
        pl.BlockSpec((1, HEAD_DIM), const2),
        pl.BlockSpec((1, HEAD_DIM), const2),
    ]
    args = [zv, zv, zv, bias, q_w.reshape(1, HEAD_DIM), k_w.reshape(1, HEAD_DIM)]
    if sink is not None:
        win = sub + 2 * radius
        row = jnp.arange(sub)
        col = jnp.where(row + 2 * radius + 1 < win, row + 2 * radius + 1, row - 1)
        at_col = jnp.arange(win)[None, :] == col[:, None]
        in_specs.append(pl.BlockSpec((n_heads, sub, win), const3))
        args.append(jnp.where(at_col[None], sink.astype(jnp.float32)[:, None, None], NEG_INF))
    omap = lambda r, i: (i, r)
    out_specs = [pl.BlockSpec((tq, res_per_step * n_heads * HEAD_DIM), omap)]
    out_shape = [jax.ShapeDtypeStruct((rows, dil * n_heads * HEAD_DIM), jnp.bfloat16)]
    if with_lse:
        out_specs.append(pl.BlockSpec((tq, res_per_step * LANES), omap))
        out_shape.append(jax.ShapeDtypeStruct((rows, dil * LANES), jnp.float32))
    kern = functools.partial(
        _attn_kernel, tq=tq, sub=sub, radius=radius, heads=heads, res_per_step=res_per_step,
        with_sink=sink is not None, with_lse=with_lse, np_rows=np_tok // dil,
        len_p=seq_p // dil, len_s=seq_s // dil)
    outs = pl.pallas_call(
        kern, grid=(dil // res_per_step, nq), in_specs=in_specs, out_specs=out_specs,
        out_shape=out_shape, compiler_params=_cparams(("parallel", "arbitrary")),
        name=f"band_attn_d{dil}_r{radius}",
    )(*args)
    o = outs[0].reshape(ntok, n_heads * HEAD_DIM)
    if with_lse:
        return o, outs[1].reshape(ntok, LANES)
    return o


def _merge_kernel(xp_ref, xs_ref, oa0_ref, oa1_ref, oa2_ref, l0_ref, l1_ref, l2_ref, ob_ref, ga0_ref,
                  ga1_ref, gb0_ref, gb1_ref, wpa_ref, wpb_ref, wo_ref, g2_ref, wr_ref, x1p_ref,
                  x1s_ref, h2p_ref, h2s_ref, aff_ref, *, tm, n_p):
    is_prompt = pl.program_id(0) < n_p
    ga = jnp.concatenate([ga0_ref[...], ga1_ref[...]], axis=1)
    gb = jnp.concatenate([gb0_ref[...], gb1_ref[...]], axis=1)
    l0, l1, l2 = l0_ref[...], l1_ref[...], l2_ref[...]
    mx = jnp.maximum(jnp.maximum(l0, l1), l2)
    e0, e1, e2 = jnp.exp(l0 - mx), jnp.exp(l1 - mx), jnp.exp(l2 - mx)
    den = e0 + e1 + e2
    w0, w1, w2 = e0 / den, e1 / den, e2 / den
    parts = []
    for h in range(A_HEADS):
        cs = slice(h * HEAD_DIM, (h + 1) * HEAD_DIM)
        parts.append(w0[:, h:h + 1] * oa0_ref[:, cs].astype(jnp.float32)
                     + w1[:, h:h + 1] * oa1_ref[:, cs].astype(jnp.float32)
                     + w2[:, h:h + 1] * oa2_ref[:, cs].astype(jnp.float32))
    o_a = jnp.concatenate(parts, axis=1).astype(jnp.bfloat16)
    pa = jnp.dot(o_a, wpa_ref[...], preferred_element_type=jnp.float32)
    pb = jnp.dot(ob_ref[...], wpb_ref[...], preferred_element_type=jnp.float32)
    merged = (jax.nn.sigmoid(ga.astype(jnp.float32)) * pa
              + jax.nn.sigmoid(gb.astype(jnp.float32)) * pb)
    delta = jnp.dot(merged.astype(jnp.bfloat16), wo_ref[...], preferred_element_type=jnp.float32)

    def finish(x_ref, x1_ref, h2_ref):
        x1 = x_ref[...] + delta
        x1_ref[...] = x1
        ms = jnp.mean(x1 * x1, axis=-1, keepdims=True)
        h2 = x1 * lax.rsqrt(ms + NORM_EPS) * g2_ref[...]
        h2_ref[...] = h2
        logits = lax.dot_general(wr_ref[...], h2.astype(jnp.bfloat16), (((1,), (1,)), ((), ())),
                                 preferred_element_type=jnp.float32)
        mx2 = jnp.max(logits, axis=0, keepdims=True)
        ex = jnp.exp(logits - mx2)
        aff = ex / jnp.sum(ex, axis=0, keepdims=True)
        for j in range(tm // LANES):
            aff_ref[j] = aff[:, j * LANES:(j + 1) * LANES]

    pl.when(is_prompt)(lambda: finish(xp_ref, x1p_ref, h2p_ref))
    pl.when(jnp.logical_not(is_prompt))(lambda: finish(xs_ref, x1s_ref, h2s_ref))


def _merge(x_p, x_s, oas, lses, ob, z, wpa, wpb, wo, g2, wr_t, tm):
    d = x_p.shape[1]
    n_p = x_p.shape[0] // tm
    n = x_p.shape[0] + x_s.shape[0]
    n_exp = wr_t.shape[0]
    half = d // 2
    assert (2 * QKV_COLS) % half == 0
    ga_blk = 2 * QKV_COLS // half
    gate = lambda k: pl.BlockSpec((tm, half), lambda i: (i, ga_blk + k))
    row = lambda i: (i, 0)
    const = lambda i: (0, 0)
    row_p = lambda i: (jnp.minimum(i, n_p - 1), 0)
    row_s = lambda i: (jnp.maximum(i - n_p, 0), 0)
    in_specs = [
        pl.BlockSpec((tm, d), row_p),
        pl.BlockSpec((tm, d), row_s),
        pl.BlockSpec((tm, A_OUT), row), pl.BlockSpec((tm, A_OUT), row), pl.BlockSpec((tm, A_OUT), row),
        pl.BlockSpec((tm, LANES), row), pl.BlockSpec((tm, LANES), row), pl.BlockSpec((tm, LANES), row),
        pl.BlockSpec((tm, B_Q), row),
        gate(0), gate(1), gate(2), gate(3),
        pl.BlockSpec((A_OUT, d), const),
        pl.BlockSpec((B_Q, d), const),
        pl.BlockSpec((d, d), const),
        pl.BlockSpec((1, d), const),
        pl.BlockSpec((n_exp, d), const),
    ]
    out_specs = [
        pl.BlockSpec((tm, d), row_p),
        pl.BlockSpec((tm, d), row_s),
        pl.BlockSpec((tm, d), row_p),
        pl.BlockSpec((tm, d), row_s),
        pl.BlockSpec((tm // LANES, n_exp, LANES), lambda i: (i, 0, 0)),
    ]
    out_shape = [
        jax.ShapeDtypeStruct(x_p.shape, jnp.float32),
        jax.ShapeDtypeStruct(x_s.shape, jnp.float32),
        jax.ShapeDtypeStruct(x_p.shape, jnp.float32),
        jax.ShapeDtypeStruct(x_s.shape, jnp.float32),
        jax.ShapeDtypeStruct((n // LANES, n_exp, LANES), jnp.float32),
    ]
    return pl.pallas_call(
        functools.partial(_merge_kernel, tm=tm, n_p=n_p),
        grid=(n // tm,), in_specs=in_specs, out_specs=out_specs, out_shape=out_shape,
        compiler_params=_cparams(("arbitrary",)),
        name="merge_proj_router",
    )(x_p, x_s, oas[0], oas[1], oas[2], lses[0], lses[1], lses[2], ob, z, z, z, z, wpa, wpb, wo,
      g2.reshape(1, d), wr_t)


def _prefix_counts(flag_f32, tri, tot_scr, off_scr, nc):
    n_exp = flag_f32.shape[1]
    incl = jnp.dot(flag_f32.astype(jnp.bfloat16).reshape(nc * n_exp, LANES), tri,
                   preferred_element_type=jnp.float32).reshape(nc, n_exp, LANES)
    tot_scr[...] = jnp.broadcast_to(incl[:, :, LANES - 1:LANES], (nc, n_exp, LANES))

    def body(c, run):
        off_scr[c] = run
        return run + tot_scr[c]

    lax.fori_loop(0, nc, body, jnp.zeros((n_exp, LANES), jnp.float32))
    return off_scr[...] + incl - flag_f32


def _route_kernel(aff_ref, pos_ref, off_ref, tot_scr, off_scr, *, cap, nc):
    n_exp = aff_ref.shape[1]
    capf = jnp.float32(cap)

    def count(mask):
        c = jnp.sum(jnp.where(mask, 1.0, 0.0), axis=0)
        return jnp.sum(c, axis=-1, keepdims=True)

    def bit_body(k, t):
        cand = t | jnp.left_shift(jnp.int32(1), 30 - k)
        bits = pltpu.bitcast(aff_ref[...], jnp.int32)
        return jnp.where(count(bits >= cand[None]) >= capf, cand, t)

    t = lax.fori_loop(0, 31, bit_body, jnp.zeros((n_exp, 1), jnp.int32))
    bits = pltpu.bitcast(aff_ref[...], jnp.int32)
    gt = bits > t[None]
    eq = bits == t[None]
    need = capf - count(gt)
    rows = lax.broadcasted_iota(jnp.int32, (LANES, LANES), 0)
    cols = lax.broadcasted_iota(jnp.int32, (LANES, LANES), 1)
    tri = jnp.where(rows <= cols, 1.0, 0.0).astype(jnp.bfloat16)
    eq_f = jnp.where(eq, 1.0, 0.0)
    tie_rank = _prefix_counts(eq_f, tri, tot_scr, off_scr, nc)
    sel = gt | (eq & (tie_rank < need[None]))
    sel_f = jnp.where(sel, 1.0, 0.0)
    slot = _prefix_counts(sel_f, tri, tot_scr, off_scr, nc)
    pos_ref[...] = jnp.where(sel, slot, -1.0).astype(jnp.int32)
    off_ref[...] = off_scr[...].astype(jnp.int32)


def _route(aff, cap):
    nc, n_exp, _ = aff.shape
    full = pl.BlockSpec((nc, n_exp, LANES), lambda: (0, 0, 0))
    return pl.pallas_call(
        functools.partial(_route_kernel, cap=cap, nc=nc),
        in_specs=[full], out_specs=[full, full],
        out_shape=[jax.ShapeDtypeStruct((nc, n_exp, LANES), jnp.int32)] * 2,
        scratch_shapes=[pltpu.VMEM((nc, n_exp, LANES), jnp.float32)] * 2,
        compiler_params=pltpu.CompilerParams(vmem_limit_bytes=VMEM_LIMIT),
        name="route_select",
    )(aff)


def _split3(x):
    a = x.astype(jnp.bfloat16).astype(jnp.float32)
    r = x - a
    b = r.astype(jnp.bfloat16).astype(jnp.float32)
    c = r - b
    return a, b, c


def _compact_kernel(off_smem, pos_ref, aff_ref, idx_ref, gate_ref, acc_scr, *, nc, n_blk, tok_base):
    e = pl.program_id(0)
    stride = nc + 1
    sub = lax.broadcasted_iota(jnp.int32, (LANES, LANES), 0)
    sub16 = lax.broadcasted_iota(jnp.int32, (16, LANES), 0)
    lane = lax.broadcasted_iota(jnp.int32, (1, LANES), 1)

    n_grp = nc // COMPACT_CHUNKS

    def chunk_off(c):
        return off_smem[e * stride + jnp.minimum(c, nc)]

    def block_body(sb, g_first):
        lo_slot = sb * LANES

        def skip_cond(g):
            return jnp.logical_and(g < n_grp, chunk_off((g + 1) * COMPACT_CHUNKS) <= lo_slot)

        g_first = lax.while_loop(skip_cond, lambda g: g + 1, g_first)
        acc_scr[...] = jnp.zeros_like(acc_scr)

        def take_cond(g):
            return jnp.logical_and(g < n_grp, chunk_off(g * COMPACT_CHUNKS) < lo_slot + LANES)

        def take(g):
            total = jnp.zeros((16, LANES), jnp.float32)
            for k in range(COMPACT_CHUNKS):
                c = g * COMPACT_CHUNKS + k
                rel = pos_ref[c, pl.ds(e, 1), :] - lo_slot
                onehot = jnp.where(sub == rel, 1.0, 0.0).astype(jnp.bfloat16)
                tok = tok_base + c * LANES + lane
                g1, g2, g3 = _split3(aff_ref[c, pl.ds(e, 1), :])
                lhs = jnp.where(sub16 == 0, (tok >> 8).astype(jnp.float32),
                      jnp.where(sub16 == 1, (tok & 255).astype(jnp.float32),
                      jnp.where(sub16 == 2, g1,
                      jnp.where(sub16 == 3, g2,
                      jnp.where(sub16 == 4, g3, 0.0))))).astype(jnp.bfloat16)
                total = total + lax.dot_general(lhs, onehot, (((1,), (1,)), ((), ())),
                                                preferred_element_type=jnp.float32)
            acc_scr[...] += total
            return g + 1

        lax.while_loop(take_cond, take, g_first)
        acc = acc_scr[...]
        idx_ref[0, pl.ds(sb, 1), :] = (acc[0:1] * 256.0 + acc[1:2]).astype(jnp.int32)
        gate_ref[0, pl.ds(sb, 1), :] = (acc[2:3] + acc[3:4]) + acc[4:5]
        return g_first

    lax.fori_loop(0, n_blk, block_body, jnp.int32(0))


def _compact(offs_flat, pos, aff, cap, tok_base):
    nc, n_exp, _ = pos.shape
    n_blk = cap // LANES
    full = pl.BlockSpec((nc, n_exp, LANES), lambda e, off: (0, 0, 0))
    out = pl.BlockSpec((1, n_blk, LANES), lambda e, off: (e, 0, 0))
    return pl.pallas_call(
        functools.partial(_compact_kernel, nc=nc, n_blk=n_blk, tok_base=tok_base),
        grid_spec=pltpu.PrefetchScalarGridSpec(
            num_scalar_prefetch=1, grid=(n_exp,), in_specs=[full, full], out_specs=[out, out],
            scratch_shapes=[pltpu.VMEM((16, LANES), jnp.float32)]),
        out_shape=[jax.ShapeDtypeStruct((n_exp, n_blk, LANES), jnp.int32),
                   jax.ShapeDtypeStruct((n_exp, n_blk, LANES), jnp.float32)],
        compiler_params=_cparams(("arbitrary",)),
        name="route_compact",
    )(offs_flat, pos, aff)


def _route_set(aff, cap, tok_base):
    nc, n_exp, _ = aff.shape
    pos, off = _route(aff, cap)
    offs = jnp.concatenate([off[:, :, 0].T, jnp.full((n_exp, 1), cap, jnp.int32)], axis=1)
    idx, gate = _compact(offs.reshape(-1), pos, aff, cap, tok_base)
    return idx.reshape(n_exp, cap), gate.reshape(n_exp, cap)


MOE_STEPS_SCATTER = 4
MOE_STEPS_GAP = 2


def _moe_kernel(idx_prev, idx_cur, idx_nxt, gate_ref, wg_hbm, wu_hbm, wd_hbm, h2_hbm, x1_hbm,
                out_hbm, xe32, xe16, acc, orow, wg_buf, wu_buf, wd_buf, sems, wsems,
                *, tm, tf, nt, n_tiles, n_f):
    del x1_hbm
    n = pl.program_id(0)
    sem_x, sem_g, sem_s = sems.at[0], sems.at[1], sems.at[2]
    last_step = n_tiles * n_f - 1
    steps_b = n_f - MOE_STEPS_SCATTER - MOE_STEPS_GAP
    rows_a = tm // MOE_STEPS_SCATTER
    rows_gap = tm // 2 // MOE_STEPS_GAP
    rows_bx = tm // 2 // steps_b
    rows_bg = tm // steps_b

    def weight_copies(step, slot):
        step = jnp.minimum(step, last_step)
        e = step // (nt * n_f)
        col = pl.multiple_of((step % n_f) * tf, tf)
        return (
            pltpu.make_async_copy(wg_hbm.at[e, :, pl.ds(col, tf)], wg_buf.at[slot], wsems.at[0, slot]),
            pltpu.make_async_copy(wu_hbm.at[e, :, pl.ds(col, tf)], wu_buf.at[slot], wsems.at[1, slot]),
            pltpu.make_async_copy(wd_hbm.at[e, pl.ds(col, tf), :], wd_buf.at[slot], wsems.at[2, slot]),
        )

    def xe_row(idx_smem, s):
        return pltpu.make_async_copy(h2_hbm.at[pl.ds(idx_smem[0, 0, s], 1)], xe32.at[pl.ds(s, 1)], sem_x)

    def residual_row(s):
        return pltpu.make_async_copy(out_hbm.at[pl.ds(idx_cur[0, 0, s], 1)], orow.at[pl.ds(s, 1)], sem_g)

    def scatter_row(idx_smem, s):
        return pltpu.make_async_copy(orow.at[pl.ds(s, 1)], out_hbm.at[pl.ds(idx_smem[0, 0, s], 1)], sem_s)

    def wait_xe():
        pltpu.make_async_copy(h2_hbm.at[pl.ds(0, tm)], xe32, sem_x).wait()

    def wait_scatter():
        pltpu.make_async_copy(orow, out_hbm.at[pl.ds(0, tm)], sem_s).wait()

    def ffn_step(f, slot):
        for c in weight_copies(n * n_f + f, slot):
            c.wait()
        for c in weight_copies(n * n_f + f + 1, 1 - slot):
            c.start()
        x = xe16[...]
        gp = jnp.dot(x, wg_buf[slot].astype(jnp.bfloat16), preferred_element_type=jnp.float32)
        up = jnp.dot(x, wu_buf[slot].astype(jnp.bfloat16), preferred_element_type=jnp.float32)
        hid = (gp * jax.nn.sigmoid(gp) * up).astype(jnp.bfloat16)
        acc[...] += jnp.dot(hid, wd_buf[slot].astype(jnp.bfloat16),
                            preferred_element_type=jnp.float32)

    def run_steps(f_lo, count, row_work):
        def body(p, _):
            for slot in (0, 1):
                ffn_step(f_lo + 2 * p + slot, slot)
                row_work(2 * p + slot)
            return 0
        lax.fori_loop(0, count // 2, body, 0)

    @pl.when(n == 0)
    def _():
        def body(s, _):
            xe_row(idx_cur, s).start()
            return 0
        lax.fori_loop(0, tm, body, 0, unroll=ROW_DMA_UNROLL)
        for c in weight_copies(0, 0):
            c.start()

    wait_xe()
    xe16[...] = xe32[...].astype(jnp.bfloat16)
    acc[...] = jnp.zeros_like(acc)

    def scatter_prev(k):
        for r in range(rows_a):
            scatter_row(idx_prev, k * rows_a + r).start()

    @pl.when(n > 0)
    def _():
        run_steps(0, MOE_STEPS_SCATTER, scatter_prev)

    @pl.when(n == 0)
    def _():
        run_steps(0, MOE_STEPS_SCATTER, lambda k: None)

    def gather_next_first_half(k):
        for r in range(rows_gap):
            xe_row(idx_nxt, k * rows_gap + r).start()

    run_steps(MOE_STEPS_SCATTER, MOE_STEPS_GAP, gather_next_first_half)

    @pl.when(n > 0)
    def _():
        wait_scatter()

    def gather_residual_and_next(k):
        for r in range(rows_bg):
            residual_row(k * rows_bg + r).start()
        for r in range(rows_bx):
            xe_row(idx_nxt, tm // 2 + k * rows_bx + r).start()

    run_steps(MOE_STEPS_SCATTER + MOE_STEPS_GAP, steps_b, gather_residual_and_next)

    pltpu.make_async_copy(out_hbm.at[pl.ds(0, tm)], orow, sem_g).wait()
    g_t = gate_ref[0].T
    for j in range(tm // LANES):
        rs = slice(j * LANES, (j + 1) * LANES)
        orow[rs, :] = orow[rs, :] + acc[rs, :] * g_t[:, j:j + 1]

    @pl.when(n == n_tiles - 1)
    def _():
        def body(s, _):
            scatter_row(idx_cur, s).start()
            return 0
        lax.fori_loop(0, tm, body, 0, unroll=ROW_DMA_UNROLL)
        wait_scatter()
        wait_xe()
        for c in weight_copies(last_step, 0):
            c.wait()


def _moe(idx, gate, h2, x1, wg, wu, wd, tm, tf):
    n_exp, slots = idx.shape
    ntok, d = x1.shape
    ff = wg.shape[2]
    nt = slots // tm
    n_f = ff // tf
    n_tiles = n_exp * nt
    steps_b = n_f - MOE_STEPS_SCATTER - MOE_STEPS_GAP
    assert n_f % 2 == 0 and steps_b > 0 and steps_b % 2 == 0 and (tm // 2) % steps_b == 0
    idx3 = idx.reshape(n_tiles, 1, tm)
    gate3 = gate.reshape(n_exp, slots // LANES, LANES)
    smem = pltpu.MemorySpace.SMEM
    hbm = pl.BlockSpec(memory_space=pl.ANY)
    in_specs = [
        pl.BlockSpec((1, 1, tm), lambda n: (jnp.maximum(n - 1, 0), 0, 0), memory_space=smem),
        pl.BlockSpec((1, 1, tm), lambda n: (n, 0, 0), memory_space=smem),
        pl.BlockSpec((1, 1, tm), lambda n: (jnp.minimum(n + 1, n_tiles - 1), 0, 0), memory_space=smem),
        pl.BlockSpec((1, tm // LANES, LANES), lambda n: (n // nt, n % nt, 0)),
        hbm, hbm, hbm, hbm, hbm,
    ]
    return pl.pallas_call(
        functools.partial(_moe_kernel, tm=tm, tf=tf, nt=nt, n_tiles=n_tiles, n_f=n_f),
        grid=(n_tiles,), in_specs=in_specs, out_specs=hbm,
        out_shape=jax.ShapeDtypeStruct((ntok, d), jnp.float32),
        scratch_shapes=[
            pltpu.VMEM((tm, d), jnp.float32),
            pltpu.VMEM((tm, d), jnp.bfloat16),
            pltpu.VMEM((tm, d), jnp.float32),
            pltpu.VMEM((tm, d), jnp.float32),
            pltpu.VMEM((2, d, tf), jnp.float32),
            pltpu.VMEM((2, d, tf), jnp.float32),
            pltpu.VMEM((2, tf, d), jnp.float32),
            pltpu.SemaphoreType.DMA((3,)),
            pltpu.SemaphoreType.DMA((3, 2)),
        ],
        input_output_aliases={8: 0},
        compiler_params=_cparams(("arbitrary",)),
        name="expert_ffn",
    )(idx3, idx3, idx3, gate3, wg, wu, wd, h2, x1)


def _pick(n, candidates):
    for c in candidates:
        if n % c == 0:
            return c
    raise ValueError(f"no tile among {candidates} divides {n}")


def _layer(x_p, x_s, seq_p, seq_s, rel_table, norm_mix_g, w_in, q_norm_a, k_norm_a, q_norm_b,
           k_norm_b, sink_b, w_proj_a, w_proj_b, w_out, norm_ffn_g, w_router, w_gate_e, w_up_e,
           w_down_e):
    np_tok, d = x_p.shape
    ns_tok = x_s.shape[0]
    ntok = np_tok + ns_tok
    bf = jnp.bfloat16
    tm_in = _pick(math.gcd(np_tok, ns_tok), (1024, 512, 256))

    def group_cols(g):
        return [w_in[:, s * A_QKV + g * GROUP_COLS: s * A_QKV + (g + 1) * GROUP_COLS] for s in range(3)]

    w_nat = jnp.concatenate(group_cols(0) + [w_in[:, 3 * A_QKV:]], axis=1).astype(bf)
    z = _inproj(x_p, x_s, norm_mix_g, w_nat, tm_in, _pick(w_nat.shape[1], (1024, 512)))

    heads_a = tuple((h * HEAD_DIM, GROUP_COLS + h * HEAD_DIM, 2 * GROUP_COLS + h * HEAD_DIM)
                    for h in range(A_HEADS))
    heads_b = tuple((h * HEAD_DIM, B_Q + (h // B_GROUP) * HEAD_DIM,
                     B_Q + B_KV + (h // B_GROUP) * HEAD_DIM) for h in range(B_Q_HEADS))
    oas, lses = [], []
    for g in range(N_DIL_GROUPS):
        dil = DIL_RATES[g]
        radius = (DIL_WINDOWS[g] // 2) // dil
        tq = min(ATTN_TQ, seq_p // dil, seq_s // dil)
        bias = _bias_tile(rel_table[:, g * A_HEADS:(g + 1) * A_HEADS], dil, radius,
                          min(ATTN_SUB, tq))
        if dil == 1:
            zv = z
        else:
            w_g = jnp.concatenate(group_cols(g), axis=1).astype(bf)
            zv = _inproj(x_p, x_s, norm_mix_g, w_g, tm_in, w_g.shape[1], dil=dil)
        o, lse = _banded_attention(
            zv, dil=dil, col0=0, radius=radius, tq=tq, heads=heads_a,
            res_per_step=ATTN_RES_PER_STEP if (dil > 1 and tq <= ATTN_SUB) else 1, bias=bias,
            q_w=q_norm_a, k_w=k_norm_a, sink=None, with_lse=True, np_tok=np_tok, seq_p=seq_p,
            seq_s=seq_s)
        oas.append(o)
        lses.append(lse)
    tq_b = min(ATTN_TQ, seq_p, seq_s)
    bias_b = _bias_tile(rel_table[:, N_DIL_GROUPS * A_HEADS:], 1, B_RADIUS, min(ATTN_SUB_B, tq_b))
    ob = _banded_attention(
        z, dil=1, col0=QKV_COLS, radius=B_RADIUS, tq=tq_b, heads=heads_b, res_per_step=1,
        bias=bias_b, q_w=q_norm_b, k_w=k_norm_b, sink=sink_b, with_lse=False, np_tok=np_tok,
        seq_p=seq_p, seq_s=seq_s)

    x1_p, x1_s, h2_p, h2_s, aff = _merge(
        x_p, x_s, oas, lses, ob, z, w_proj_a.astype(bf), w_proj_b.astype(bf), w_out.astype(bf),
        norm_ffn_g, w_router.T.astype(bf), _pick(math.gcd(np_tok, ns_tok), (256,)))

    n_exp = w_router.shape[1]
    tf = _pick(w_gate_e.shape[2], (256, 128))
    outs = []
    for x1, h2, aff_set in ((x1_p, h2_p, aff[:np_tok // LANES]), (x1_s, h2_s, aff[np_tok // LANES:])):
        cap = max(1, EC_CAPACITY * x1.shape[0] // n_exp)
        idx, gate = _route_set(aff_set, cap, 0)
        outs.append(_moe(idx, gate, h2, x1, w_gate_e, w_up_e, w_down_e, _pick(cap, (1024,)), tf))
    return outs


def kernel(x_prompt, x_sample, rel_table, norm_mix_g, w_in, q_norm_a, k_norm_a, q_norm_b, k_norm_b,
           sink_b, w_proj_a, w_proj_b, w_out, norm_ffn_g, w_router, w_gate_e, w_up_e, w_down_e):
    bp, sp, d = x_prompt.shape
    bs, ss, _ = x_sample.shape
    x_p = x_prompt.reshape(bp * sp, d)
    x_s = x_sample.reshape(bs * ss, d)
    for l in range(norm_mix_g.shape[0]):
        x_p, x_s = _layer(x_p, x_s, sp, ss, rel_table, norm_mix_g[l], w_in[l], q_norm_a[l],
                          k_norm_a[l], q_norm_b[l], k_norm_b[l], sink_b[l], w_proj_a[l],
                          w_proj_b[l], w_out[l], norm_ffn_g[l], w_router[l], w_gate_e[l],
                          w_up_e[l], w_down_e[l])
    return x_p.reshape(bp, sp, d), x_s.reshape(bs, ss, d)
```

```python
import functools
import math

import jax
import jax.numpy as jnp
from jax import lax
from jax.experimental import pallas as pl
from jax.experimental.pallas import tpu as pltpu

HEAD_DIM = 128
DIL_WINDOWS = (128, 512, 2048)
DIL_RATES = (1, 4, 16)
N_DIL_GROUPS = 3
A_HEADS = 4
B_Q_HEADS = 8
B_KV_HEADS = 2
B_GROUP = B_Q_HEADS // B_KV_HEADS
B_RADIUS = 128
REL_BUCKETS = 32
REL_MAX_DIST = 1024
EC_CAPACITY = 2
NORM_EPS = 1e-6
NEG_INF = -1e30

A_QKV = N_DIL_GROUPS * A_HEADS * HEAD_DIM
A_OUT = A_HEADS * HEAD_DIM
B_Q = B_Q_HEADS * HEAD_DIM
B_KV = B_KV_HEADS * HEAD_DIM

LANES = 128
GROUP_COLS = A_HEADS * HEAD_DIM
VMEM_LIMIT = 60 * 1024 * 1024
INPROJ_CHUNK = 256
QKV_COLS = 3 * GROUP_COLS
ATTN_TQ = 512
ATTN_SUB = 128
ATTN_SUB_B = 128
ATTN_RES_PER_STEP = 4
COMPACT_CHUNKS = 8
ROW_DMA_UNROLL = 8


def _cparams(sem, vmem=VMEM_LIMIT):
    return pltpu.CompilerParams(dimension_semantics=sem, vmem_limit_bytes=vmem)


def _inproj_kernel(xp_ref, xs_ref, g_ref, w_ref, z_ref, h_scr, *, tm, n_p):
    def normalise(x_ref):
        x = x_ref[...]
        ms = jnp.mean(x * x, axis=-1, keepdims=True)
        h_scr[...] = (x * lax.rsqrt(ms + NORM_EPS) * g_ref[...]).astype(jnp.bfloat16)

    first_col = pl.program_id(1) == 0
    is_prompt = pl.program_id(0) < n_p
    pl.when(jnp.logical_and(first_col, is_prompt))(lambda: normalise(xp_ref))
    pl.when(jnp.logical_and(first_col, jnp.logical_not(is_prompt)))(lambda: normalise(xs_ref))

    z_ref[...] = jnp.dot(h_scr[...], w_ref[...],
                         preferred_element_type=jnp.float32).astype(jnp.bfloat16)


def _inproj_strided_kernel(xp_ref, xs_ref, g_ref, w_ref, z_ref, zs, *, dil, tm, n_p):
    tn = w_ref.shape[1]
    chunk = INPROJ_CHUNK
    rows = chunk // dil

    def run(x_ref):
        for ci in range(tm // chunk):
            x = x_ref[ci * chunk:(ci + 1) * chunk, :]
            ms = jnp.mean(x * x, axis=-1, keepdims=True)
            h = (x * lax.rsqrt(ms + NORM_EPS) * g_ref[...]).astype(jnp.bfloat16)
            z = jnp.dot(h, w_ref[...], preferred_element_type=jnp.float32)
            slab = zs.at[ci % 2]
            for c in range(tn // LANES):
                slab[c] = z[:, c * LANES:(c + 1) * LANES]
            for r in range(dil):
                for c in range(tn // LANES):
                    z_ref[ci * rows:(ci + 1) * rows, r * tn + c * LANES:r * tn + (c + 1) * LANES] = (
                        slab[c, pl.ds(r, rows, stride=dil), :].astype(jnp.bfloat16))

    is_prompt = pl.program_id(0) < n_p
    pl.when(is_prompt)(lambda: run(xp_ref))
    pl.when(jnp.logical_not(is_prompt))(lambda: run(xs_ref))


def _inproj(x_p, x_s, g, w_bf16, tm, tn, dil=1):
    d = x_p.shape[1]
    n_p = x_p.shape[0] // tm
    n = x_p.shape[0] + x_s.shape[0]
    cols = w_bf16.shape[1]
    assert dil == 1 or tn == cols
    if dil == 1:
        kern = functools.partial(_inproj_kernel, tm=tm, n_p=n_p)
        scratch = [pltpu.VMEM((tm, d), jnp.bfloat16)]
    else:
        kern = functools.partial(_inproj_strided_kernel, dil=dil, tm=tm, n_p=n_p)
        scratch = [pltpu.VMEM((2, tn // LANES, INPROJ_CHUNK, LANES), jnp.float32)]
    return pl.pallas_call(
        kern,
        grid=(n // tm, cols // tn),
        in_specs=[
            pl.BlockSpec((tm, d), lambda i, j: (jnp.minimum(i, n_p - 1), 0)),
            pl.BlockSpec((tm, d), lambda i, j: (jnp.maximum(i - n_p, 0), 0)),
            pl.BlockSpec((1, d), lambda i, j: (0, 0)),
            pl.BlockSpec((d, tn), lambda i, j: (0, j)),
        ],
        out_specs=pl.BlockSpec((tm // dil, dil * tn), lambda i, j: (i, j)),
        out_shape=jax.ShapeDtypeStruct((n // dil, dil * cols), jnp.bfloat16),
        scratch_shapes=scratch,
        compiler_params=_cparams(("parallel", "arbitrary")),
        name=f"inproj_d{dil}",
    )(x_p, x_s, g.reshape(1, d), w_bf16)


def _t5_bucket(rel):
    half = REL_BUCKETS // 2
    max_exact = half // 2
    n = jnp.abs(rel)
    base = jnp.where(rel > 0, half, 0)
    nf = jnp.maximum(n, 1).astype(jnp.float32)
    large = max_exact + (jnp.log(nf / max_exact) / math.log(REL_MAX_DIST / max_exact)
                         * (half - max_exact)).astype(jnp.int32)
    large = jnp.minimum(large, half - 1)
    return base + jnp.where(n < max_exact, n, large)


def _bias_tile(table_cols, dil, radius, tq):
    tk = tq + 2 * radius
    n_heads = table_cols.shape[1]
    rel = jnp.arange(-radius, radius + 1)
    vals = table_cols[_t5_bucket(rel * dil)].astype(jnp.float32).T
    period = tq + tk
    w = jnp.full((n_heads, period), NEG_INF, jnp.float32).at[:, :2 * radius + 1].set(vals)
    flat = jnp.tile(w, (1, tq))[:, :tq * (period - 1)]
    return flat.reshape(n_heads, tq, period - 1)[:, :, :tk]


def _head_norm(x, w):
    xf = x.astype(jnp.float32)
    ms = jnp.mean(xf * xf, axis=-1, keepdims=True)
    return xf * lax.rsqrt(ms + NORM_EPS) * w


def _attn_kernel(*refs, tq, sub, radius, heads, res_per_step, with_sink, with_lse, np_rows,
                 len_p, len_s):
    cur_ref, prev_ref, next_ref, bias_ref, qw_ref, kw_ref = refs[:6]
    pos = 6
    sink_ref = None
    if with_sink:
        sink_ref = refs[pos]
        pos += 1
    o_ref = refs[pos]
    lse_ref = refs[pos + 1] if with_lse else None

    win = sub + 2 * radius
    n_heads = len(heads)
    q0 = pl.program_id(1) * tq
    lo_p = (q0 // len_p) * len_p
    lo_s = np_rows + ((q0 - np_rows) // len_s) * len_s
    in_p = q0 < np_rows
    lo = jnp.where(in_p, lo_p, lo_s)
    hi = lo + jnp.where(in_p, len_p, len_s)
    key_iota = lax.broadcasted_iota(jnp.int32, (1, win), 1)
    lane = lax.broadcasted_iota(jnp.int32, (sub, LANES), 1)
    scale = HEAD_DIM ** -0.5
    qw = qw_ref[...]
    kw = kw_ref[...]

    def window(col):
        cs = slice(col, col + HEAD_DIM)
        return jnp.concatenate([prev_ref[:, cs], cur_ref[:, cs], next_ref[:, cs]], axis=0)

    kv_groups = {}
    for h, (_, k_off, v_off) in enumerate(heads):
        kv_groups.setdefault((k_off, v_off), []).append(h)

    def stack(pieces):
        return pieces[0] if len(pieces) == 1 else jnp.concatenate(pieces, axis=0)

    for rr in range(res_per_step):
        base = rr * QKV_COLS
        lse_tiles = [jnp.zeros((sub, LANES), jnp.float32) for _ in range(tq // sub)]
        for (k_off, v_off), hs in kv_groups.items():
            kh = _head_norm(window(base + k_off), kw).astype(jnp.bfloat16)
            vh = window(base + v_off)
            qs = [(_head_norm(cur_ref[:, base + heads[h][0]:base + heads[h][0] + HEAD_DIM], qw)
                   * scale).astype(jnp.bfloat16) for h in hs]
            bias = stack([bias_ref[h] for h in hs])
            if with_sink:
                sink = stack([sink_ref[h] for h in hs])
                is_sink = sink > NEG_INF
            for a in range(tq // sub):
                kpos = q0 + a * sub - radius + key_iota
                valid = (kpos >= lo) & (kpos < hi)
                s = lax.dot_general(stack([q[a * sub:(a + 1) * sub] for q in qs]),
                                    kh[a * sub:a * sub + win], (((1,), (1,)), ((), ())),
                                    preferred_element_type=jnp.float32)
                s = jnp.where(valid, s + bias, NEG_INF)
                if with_sink:
                    s = jnp.maximum(s, sink)
                m = jnp.max(s, axis=-1, keepdims=True)
                p = jnp.exp(s - m)
                l = jnp.sum(p, axis=-1, keepdims=True)
                if with_sink:
                    p = jnp.where(is_sink, 0.0, p)
                o = jnp.dot(p.astype(jnp.bfloat16), vh[a * sub:a * sub + win],
                            preferred_element_type=jnp.float32) / l
                lse = m + jnp.log(l)
                for gi, h in enumerate(hs):
                    oc = (rr * n_heads + h) * HEAD_DIM
                    o_ref[a * sub:(a + 1) * sub, oc:oc + HEAD_DIM] = (
                        o[gi * sub:(gi + 1) * sub].astype(o_ref.dtype))
                    if with_lse:
                        lse_tiles[a] = jnp.where(lane == h, lse[gi * sub:(gi + 1) * sub],
                                                 lse_tiles[a])
        if with_lse:
            for a in range(tq // sub):
                lse_ref[a * sub:(a + 1) * sub, rr * LANES:(rr + 1) * LANES] = lse_tiles[a]


def _banded_attention(zv, *, dil, col0, radius, tq, heads, res_per_step, bias, q_w, k_w, sink,
                      with_lse, np_tok, seq_p, seq_s):
    rows = zv.shape[0]
    ntok = rows * dil
    n_heads = len(heads)
    assert col0 % QKV_COLS == 0 and dil % res_per_step == 0
    assert dil == 1 or zv.shape[1] == dil * QKV_COLS
    blk0 = col0 // QKV_COLS
    nq = rows // tq
    hb = tq // radius
    n_halo = rows // radius
    sub = bias.shape[1]
    wblk = res_per_step * QKV_COLS
    const2 = lambda r, i: (0, 0)
    const3 = lambda r, i: (0, 0, 0)
    in_specs = [
        pl.BlockSpec((tq, wblk), lambda r, i: (i, blk0 + r)),
        pl.BlockSpec((radius, wblk), lambda r, i: (jnp.maximum(i * hb - 1, 0), blk0 + r)),
        pl.BlockSpec((radius, wblk), lambda r, i: (jnp.minimum((i + 1) * hb, n_halo - 1), blk0 + r)),
        pl.BlockSpec((n_heads, sub, sub + 2 * radius), const3),
        pl.BlockSpec((1, HEAD_DIM), const2),
        pl.BlockSpec((1, HEAD_DIM), const2),
    ]
    args = [zv, zv, zv, bias, q_w.reshape(1, HEAD_DIM), k_w.reshape(1, HEAD_DIM)]
    if sink is not None:
        win = sub + 2 * radius
        row = jnp.arange(sub)
        col = jnp.where(row + 2 * radius + 1 < win, row + 2 * radius + 1, row - 1)
        at_col = jnp.arange(win)[None, :] == col[:, None]
        in_specs.append(pl.BlockSpec((n_heads, sub, win), const3))
        args.append(jnp.where(at_col[None], sink.astype(jnp.float32)[:, None, None], NEG_INF))
    omap = lambda r, i: (i, r)
    out_specs = [pl.BlockSpec((tq, res_per_step * n_heads * HEAD_DIM), omap)]
    out_shape = [jax.ShapeDtypeStruct((rows, dil * n_heads * HEAD_DIM), jnp.bfloat16)]
    if with_lse:
        out_specs.append(pl.BlockSpec((tq, res_per_step * LANES), omap))
        out_shape.append(jax.ShapeDtypeStruct((rows, dil * LANES), jnp.float32))
    kern = functools.partial(
        _attn_kernel, tq=tq, sub=sub, radius=radius, heads=heads, res_per_step=res_per_step,
        with_sink=sink is not None, with_lse=with_lse, np_rows=np_tok // dil,
        len_p=seq_p // dil, len_s=seq_s // dil)
    outs = pl.pallas_call(
        kern, grid=(dil // res_per_step, nq), in_specs=in_specs, out_specs=out_specs,
        out_shape=out_shape, compiler_params=_cparams(("parallel", "arbitrary")),
        name=f"band_attn_d{dil}_r{radius}",
    )(*args)
    return tuple(outs) if with_lse else outs[0]


def _merge_kernel(xp_ref, xs_ref, oa0_ref, oa1_ref, oa2_ref, l0_ref, l1_ref, l2_ref, ob_ref, ga0_ref,
                  ga1_ref, gb0_ref, gb1_ref, wpa_ref, wpb_ref, wo_ref, g2_ref, wr_ref, x1p_ref,
                  x1s_ref, h2p_ref, h2s_ref, aff_ref, oa_scr, l_scr, *, tm, n_p):
    is_prompt = pl.program_id(0) < n_p
    ga = jnp.concatenate([ga0_ref[...], ga1_ref[...]], axis=1)
    gb = jnp.concatenate([gb0_ref[...], gb1_ref[...]], axis=1)

    def natural_order(g, o_ref, l_ref):
        dil = DIL_RATES[g]
        if dil == 1:
            return ([o_ref[:, h * HEAD_DIM:(h + 1) * HEAD_DIM].astype(jnp.float32)
                     for h in range(A_HEADS)], l_ref[...])
        rows = tm // dil
        for r in range(dil):
            for h in range(A_HEADS):
                c0 = r * A_OUT + h * HEAD_DIM
                oa_scr[g, h, pl.ds(r, rows, stride=dil), :] = (
                    o_ref[:, c0:c0 + HEAD_DIM].astype(jnp.float32))
            l_scr[g, pl.ds(r, rows, stride=dil), :] = l_ref[:, r * LANES:(r + 1) * LANES]
        return [oa_scr[g, h] for h in range(A_HEADS)], l_scr[g]

    (o0, l0), (o1, l1), (o2, l2) = (natural_order(0, oa0_ref, l0_ref),
                                    natural_order(1, oa1_ref, l1_ref),
                                    natural_order(2, oa2_ref, l2_ref))
    mx = jnp.maximum(jnp.maximum(l0, l1), l2)
    e0, e1, e2 = jnp.exp(l0 - mx), jnp.exp(l1 - mx), jnp.exp(l2 - mx)
    den = e0 + e1 + e2
    w0, w1, w2 = e0 / den, e1 / den, e2 / den
    parts = [w0[:, h:h + 1] * o0[h] + w1[:, h:h + 1] * o1[h] + w2[:, h:h + 1] * o2[h]
             for h in range(A_HEADS)]
    o_a = jnp.concatenate(parts, axis=1).astype(jnp.bfloat16)
    pa = jnp.dot(o_a, wpa_ref[...], preferred_element_type=jnp.float32)
    pb = jnp.dot(ob_ref[...], wpb_ref[...], preferred_element_type=jnp.float32)
    merged = (jax.nn.sigmoid(ga.astype(jnp.float32)) * pa
              + jax.nn.sigmoid(gb.astype(jnp.float32)) * pb)
    delta = jnp.dot(merged.astype(jnp.bfloat16), wo_ref[...], preferred_element_type=jnp.float32)

    def finish(x_ref, x1_ref, h2_ref):
        x1 = x_ref[...] + delta
        x1_ref[...] = x1
        ms = jnp.mean(x1 * x1, axis=-1, keepdims=True)
        h2 = x1 * lax.rsqrt(ms + NORM_EPS) * g2_ref[...]
        h2_ref[...] = h2
        logits = lax.dot_general(wr_ref[...], h2.astype(jnp.bfloat16), (((1,), (1,)), ((), ())),
                                 preferred_element_type=jnp.float32)
        mx2 = jnp.max(logits, axis=0, keepdims=True)
        ex = jnp.exp(logits - mx2)
        aff = ex / jnp.sum(ex, axis=0, keepdims=True)
        for j in range(tm // LANES):
            aff_ref[j] = aff[:, j * LANES:(j + 1) * LANES]

    pl.when(is_prompt)(lambda: finish(xp_ref, x1p_ref, h2p_ref))
    pl.when(jnp.logical_not(is_prompt))(lambda: finish(xs_ref, x1s_ref, h2s_ref))


def _merge(x_p, x_s, oas, lses, ob, z, wpa, wpb, wo, g2, wr_t, tm):
    d = x_p.shape[1]
    n_p = x_p.shape[0] // tm
    n = x_p.shape[0] + x_s.shape[0]
    n_exp = wr_t.shape[0]
    half = d // 2
    assert (2 * QKV_COLS) % half == 0
    ga_blk = 2 * QKV_COLS // half
    gate = lambda k: pl.BlockSpec((tm, half), lambda i: (i, ga_blk + k))
    row = lambda i: (i, 0)
    const = lambda i: (0, 0)
    row_p = lambda i: (jnp.minimum(i, n_p - 1), 0)
    row_s = lambda i: (jnp.maximum(i - n_p, 0), 0)
    in_specs = [
        pl.BlockSpec((tm, d), row_p),
        pl.BlockSpec((tm, d), row_s),
        *[pl.BlockSpec((tm // dl, dl * A_OUT), row) for dl in DIL_RATES],
        *[pl.BlockSpec((tm // dl, dl * LANES), row) for dl in DIL_RATES],
        pl.BlockSpec((tm, B_Q), row),
        gate(0), gate(1), gate(2), gate(3),
        pl.BlockSpec((A_OUT, d), const),
        pl.BlockSpec((B_Q, d), const),
        pl.BlockSpec((d, d), const),
        pl.BlockSpec((1, d), const),
        pl.BlockSpec((n_exp, d), const),
    ]
    out_specs = [
        pl.BlockSpec((tm, d), row_p),
        pl.BlockSpec((tm, d), row_s),
        pl.BlockSpec((tm, d), row_p),
        pl.BlockSpec((tm, d), row_s),
        pl.BlockSpec((tm // LANES, n_exp, LANES), lambda i: (i, 0, 0)),
    ]
    out_shape = [
        jax.ShapeDtypeStruct(x_p.shape, jnp.float32),
        jax.ShapeDtypeStruct(x_s.shape, jnp.float32),
        jax.ShapeDtypeStruct(x_p.shape, jnp.float32),
        jax.ShapeDtypeStruct(x_s.shape, jnp.float32),
        jax.ShapeDtypeStruct((n // LANES, n_exp, LANES), jnp.float32),
    ]
    return pl.pallas_call(
        functools.partial(_merge_kernel, tm=tm, n_p=n_p),
        grid=(n // tm,), in_specs=in_specs, out_specs=out_specs, out_shape=out_shape,
        scratch_shapes=[pltpu.VMEM((N_DIL_GROUPS, A_HEADS, tm, LANES), jnp.float32),
                        pltpu.VMEM((N_DIL_GROUPS, tm, LANES), jnp.float32)],
        compiler_params=_cparams(("arbitrary",)),
        name="merge_proj_router",
    )(x_p, x_s, oas[0], oas[1], oas[2], lses[0], lses[1], lses[2], ob, z, z, z, z, wpa, wpb, wo,
      g2.reshape(1, d), wr_t)


def _prefix_counts(flag_f32, tri, tot_scr, off_scr, nc):
    n_exp = flag_f32.shape[1]
    incl = jnp.dot(flag_f32.astype(jnp.bfloat16).reshape(nc * n_exp, LANES), tri,
                   preferred_element_type=jnp.float32).reshape(nc, n_exp, LANES)
    tot_scr[...] = jnp.broadcast_to(incl[:, :, LANES - 1:LANES], (nc, n_exp, LANES))

    def body(c, run):
        off_scr[c] = run
        return run + tot_scr[c]

    lax.fori_loop(0, nc, body, jnp.zeros((n_exp, LANES), jnp.float32))
    return off_scr[...] + incl - flag_f32


def _route_kernel(aff_ref, pos_ref, off_ref, tot_scr, off_scr, *, cap, nc):
    n_exp = aff_ref.shape[1]
    capf = jnp.float32(cap)

    def count(mask):
        c = jnp.sum(jnp.where(mask, 1.0, 0.0), axis=0)
        return jnp.sum(c, axis=-1, keepdims=True)

    def bit_body(k, t):
        cand = t | jnp.left_shift(jnp.int32(1), 30 - k)
        bits = pltpu.bitcast(aff_ref[...], jnp.int32)
        return jnp.where(count(bits >= cand[None]) >= capf, cand, t)

    t = lax.fori_loop(0, 31, bit_body, jnp.zeros((n_exp, 1), jnp.int32))
    bits = pltpu.bitcast(aff_ref[...], jnp.int32)
    gt = bits > t[None]
    eq = bits == t[None]
    need = capf - count(gt)
    rows = lax.broadcasted_iota(jnp.int32, (LANES, LANES), 0)
    cols = lax.broadcasted_iota(jnp.int32, (LANES, LANES), 1)
    tri = jnp.where(rows <= cols, 1.0, 0.0).astype(jnp.bfloat16)
    eq_f = jnp.where(eq, 1.0, 0.0)
    tie_rank = _prefix_counts(eq_f, tri, tot_scr, off_scr, nc)
    sel = gt | (eq & (tie_rank < need[None]))
    sel_f = jnp.where(sel, 1.0, 0.0)
    slot = _prefix_counts(sel_f, tri, tot_scr, off_scr, nc)
    pos_ref[...] = jnp.where(sel, slot, -1.0).astype(jnp.int32)
    off_ref[...] = off_scr[...].astype(jnp.int32)


def _route(aff, cap):
    nc, n_exp, _ = aff.shape
    full = pl.BlockSpec((nc, n_exp, LANES), lambda: (0, 0, 0))
    return pl.pallas_call(
        functools.partial(_route_kernel, cap=cap, nc=nc),
        in_specs=[full], out_specs=[full, full],
        out_shape=[jax.ShapeDtypeStruct((nc, n_exp, LANES), jnp.int32)] * 2,
        scratch_shapes=[pltpu.VMEM((nc, n_exp, LANES), jnp.float32)] * 2,
        compiler_params=pltpu.CompilerParams(vmem_limit_bytes=VMEM_LIMIT),
        name="route_select",
    )(aff)


def _split3(x):
    a = x.astype(jnp.bfloat16).astype(jnp.float32)
    r = x - a
    b = r.astype(jnp.bfloat16).astype(jnp.float32)
    c = r - b
    return a, b, c


def _compact_kernel(off_smem, pos_ref, aff_ref, idx_ref, gate_ref, acc_scr, *, nc, n_blk, tok_base):
    e = pl.program_id(0)
    stride = nc + 1
    sub = lax.broadcasted_iota(jnp.int32, (LANES, LANES), 0)
    sub16 = lax.broadcasted_iota(jnp.int32, (16, LANES), 0)
    lane = lax.broadcasted_iota(jnp.int32, (1, LANES), 1)

    n_grp = nc // COMPACT_CHUNKS

    def chunk_off(c):
        return off_smem[e * stride + jnp.minimum(c, nc)]

    def block_body(sb, g_first):
        lo_slot = sb * LANES

        def skip_cond(g):
            return jnp.logical_and(g < n_grp, chunk_off((g + 1) * COMPACT_CHUNKS) <= lo_slot)

        g_first = lax.while_loop(skip_cond, lambda g: g + 1, g_first)
        acc_scr[...] = jnp.zeros_like(acc_scr)

        def take_cond(g):
            return jnp.logical_and(g < n_grp, chunk_off(g * COMPACT_CHUNKS) < lo_slot + LANES)

        def take(g):
            total = jnp.zeros((16, LANES), jnp.float32)
            for k in range(COMPACT_CHUNKS):
                c = g * COMPACT_CHUNKS + k
                rel = pos_ref[c, pl.ds(e, 1), :] - lo_slot
                onehot = jnp.where(sub == rel, 1.0, 0.0).astype(jnp.bfloat16)
                tok = tok_base + c * LANES + lane
                g1, g2, g3 = _split3(aff_ref[c, pl.ds(e, 1), :])
                lhs = jnp.where(sub16 == 0, (tok >> 8).astype(jnp.float32),
                      jnp.where(sub16 == 1, (tok & 255).astype(jnp.float32),
                      jnp.where(sub16 == 2, g1,
                      jnp.where(sub16 == 3, g2,
                      jnp.where(sub16 == 4, g3, 0.0))))).astype(jnp.bfloat16)
                total = total + lax.dot_general(lhs, onehot, (((1,), (1,)), ((), ())),
                                                preferred_element_type=jnp.float32)
            acc_scr[...] += total
            return g + 1

        lax.while_loop(take_cond, take, g_first)
        acc = acc_scr[...]
        idx_ref[0, pl.ds(sb, 1), :] = (acc[0:1] * 256.0 + acc[1:2]).astype(jnp.int32)
        gate_ref[0, pl.ds(sb, 1), :] = (acc[2:3] + acc[3:4]) + acc[4:5]
        return g_first

    lax.fori_loop(0, n_blk, block_body, jnp.int32(0))


def _compact(offs_flat, pos, aff, cap, tok_base):
    nc, n_exp, _ = pos.shape
    n_blk = cap // LANES
    full = pl.BlockSpec((nc, n_exp, LANES), lambda e, off: (0, 0, 0))
    out = pl.BlockSpec((1, n_blk, LANES), lambda e, off: (e, 0, 0))
    return pl.pallas_call(
        functools.partial(_compact_kernel, nc=nc, n_blk=n_blk, tok_base=tok_base),
        grid_spec=pltpu.PrefetchScalarGridSpec(
            num_scalar_prefetch=1, grid=(n_exp,), in_specs=[full, full], out_specs=[out, out],
            scratch_shapes=[pltpu.VMEM((16, LANES), jnp.float32)]),
        out_shape=[jax.ShapeDtypeStruct((n_exp, n_blk, LANES), jnp.int32),
                   jax.ShapeDtypeStruct((n_exp, n_blk, LANES), jnp.float32)],
        compiler_params=_cparams(("arbitrary",)),
        name="route_compact",
    )(offs_flat, pos, aff)


def _route_set(aff, cap, tok_base):
    nc, n_exp, _ = aff.shape
    pos, off = _route(aff, cap)
    offs = jnp.concatenate([off[:, :, 0].T, jnp.full((n_exp, 1), cap, jnp.int32)], axis=1)
    idx, gate = _compact(offs.reshape(-1), pos, aff, cap, tok_base)
    return idx.reshape(n_exp, cap), gate.reshape(n_exp, cap)


MOE_STEPS_SCATTER = 4
MOE_STEPS_GAP = 2


def _moe_kernel(idx_prev, idx_cur, idx_nxt, gate_ref, wg_hbm, wu_hbm, wd_hbm, h2_hbm, x1_hbm,
                out_hbm, xe32, xe16, acc, orow, wg_buf, wu_buf, wd_buf, sems, wsems,
                *, tm, tf, nt, n_tiles, n_f):
    del x1_hbm
    n = pl.program_id(0)
    sem_x, sem_g, sem_s = sems.at[0], sems.at[1], sems.at[2]
    last_step = n_tiles * n_f - 1
    steps_b = n_f - MOE_STEPS_SCATTER - MOE_STEPS_GAP
    rows_a = tm // MOE_STEPS_SCATTER
    rows_gap = tm // 2 // MOE_STEPS_GAP
    rows_bx = tm // 2 // steps_b
    rows_bg = tm // steps_b

    def weight_copies(step, slot):
        step = jnp.minimum(step, last_step)
        e = step // (nt * n_f)
        col = pl.multiple_of((step % n_f) * tf, tf)
        return (
            pltpu.make_async_copy(wg_hbm.at[e, :, pl.ds(col, tf)], wg_buf.at[slot], wsems.at[0, slot]),
            pltpu.make_async_copy(wu_hbm.at[e, :, pl.ds(col, tf)], wu_buf.at[slot], wsems.at[1, slot]),
            pltpu.make_async_copy(wd_hbm.at[e, pl.ds(col, tf), :], wd_buf.at[slot], wsems.at[2, slot]),
        )

    def xe_row(idx_smem, s):
        return pltpu.make_async_copy(h2_hbm.at[pl.ds(idx_smem[0, 0, s], 1)], xe32.at[pl.ds(s, 1)], sem_x)

    def residual_row(s):
        return pltpu.make_async_copy(out_hbm.at[pl.ds(idx_cur[0, 0, s], 1)], orow.at[pl.ds(s, 1)], sem_g)

    def scatter_row(idx_smem, s):
        return pltpu.make_async_copy(orow.at[pl.ds(s, 1)], out_hbm.at[pl.ds(idx_smem[0, 0, s], 1)], sem_s)

    def wait_xe():
        pltpu.make_async_copy(h2_hbm.at[pl.ds(0, tm)], xe32, sem_x).wait()

    def wait_scatter():
        pltpu.make_async_copy(orow, out_hbm.at[pl.ds(0, tm)], sem_s).wait()

    def ffn_step(f, slot):
        for c in weight_copies(n * n_f + f, slot):
            c.wait()
        for c in weight_copies(n * n_f + f + 1, 1 - slot):
            c.start()
        x = xe16[...]
        gp = jnp.dot(x, wg_buf[slot].astype(jnp.bfloat16), preferred_element_type=jnp.float32)
        up = jnp.dot(x, wu_buf[slot].astype(jnp.bfloat16), preferred_element_type=jnp.float32)
        hid = (gp * jax.nn.sigmoid(gp) * up).astype(jnp.bfloat16)
        acc[...] += jnp.dot(hid, wd_buf[slot].astype(jnp.bfloat16),
                            preferred_element_type=jnp.float32)

    def run_steps(f_lo, count, row_work):
        def body(p, _):
            for slot in (0, 1):
                ffn_step(f_lo + 2 * p + slot, slot)
                row_work(2 * p + slot)
            return 0
        lax.fori_loop(0, count // 2, body, 0)

    @pl.when(n == 0)
    def _():
        def body(s, _):
            xe_row(idx_cur, s).start()
            return 0
        lax.fori_loop(0, tm, body, 0, unroll=ROW_DMA_UNROLL)
        for c in weight_copies(0, 0):
            c.start()

    wait_xe()
    xe16[...] = xe32[...].astype(jnp.bfloat16)
    acc[...] = jnp.zeros_like(acc)

    def scatter_prev(k):
        for r in range(rows_a):
            scatter_row(idx_prev, k * rows_a + r).start()

    @pl.when(n > 0)
    def _():
        run_steps(0, MOE_STEPS_SCATTER, scatter_prev)

    @pl.when(n == 0)
    def _():
        run_steps(0, MOE_STEPS_SCATTER, lambda k: None)

    def gather_next_first_half(k):
        for r in range(rows_gap):
            xe_row(idx_nxt, k * rows_gap + r).start()

    run_steps(MOE_STEPS_SCATTER, MOE_STEPS_GAP, gather_next_first_half)

    @pl.when(n > 0)
    def _():
        wait_scatter()

    def gather_residual_and_next(k):
        for r in range(rows_bg):
            residual_row(k * rows_bg + r).start()
        for r in range(rows_bx):
            xe_row(idx_nxt, tm // 2 + k * rows_bx + r).start()

    run_steps(MOE_STEPS_SCATTER + MOE_STEPS_GAP, steps_b, gather_residual_and_next)

    pltpu.make_async_copy(out_hbm.at[pl.ds(0, tm)], orow, sem_g).wait()
    g_t = gate_ref[0].T
    for j in range(tm // LANES):
        rs = slice(j * LANES, (j + 1) * LANES)
        orow[rs, :] = orow[rs, :] + acc[rs, :] * g_t[:, j:j + 1]

    @pl.when(n == n_tiles - 1)
    def _():
        def body(s, _):
            scatter_row(idx_cur, s).start()
            return 0
        lax.fori_loop(0, tm, body, 0, unroll=ROW_DMA_UNROLL)
        wait_scatter()
        wait_xe()
        for c in weight_copies(last_step, 0):
            c.wait()


def _moe(idx, gate, h2, x1, wg, wu, wd, tm, tf):
    n_exp, slots = idx.shape
    ntok, d = x1.shape
    ff = wg.shape[2]
    nt = slots // tm
    n_f = ff // tf
    n_tiles = n_exp * nt
    steps_b = n_f - MOE_STEPS_SCATTER - MOE_STEPS_GAP
    assert n_f % 2 == 0 and steps_b > 0 and steps_b % 2 == 0 and (tm // 2) % steps_b == 0
    idx3 = idx.reshape(n_tiles, 1, tm)
    gate3 = gate.reshape(n_exp, slots // LANES, LANES)
    smem = pltpu.MemorySpace.SMEM
    hbm = pl.BlockSpec(memory_space=pl.ANY)
    in_specs = [
        pl.BlockSpec((1, 1, tm), lambda n: (jnp.maximum(n - 1, 0), 0, 0), memory_space=smem),
        pl.BlockSpec((1, 1, tm), lambda n: (n, 0, 0), memory_space=smem),
        pl.BlockSpec((1, 1, tm), lambda n: (jnp.minimum(n + 1, n_tiles - 1), 0, 0), memory_space=smem),
        pl.BlockSpec((1, tm // LANES, LANES), lambda n: (n // nt, n % nt, 0)),
        hbm, hbm, hbm, hbm, hbm,
    ]
    return pl.pallas_call(
        functools.partial(_moe_kernel, tm=tm, tf=tf, nt=nt, n_tiles=n_tiles, n_f=n_f),
        grid=(n_tiles,), in_specs=in_specs, out_specs=hbm,
        out_shape=jax.ShapeDtypeStruct((ntok, d), jnp.float32),
        scratch_shapes=[
            pltpu.VMEM((tm, d), jnp.float32),
            pltpu.VMEM((tm, d), jnp.bfloat16),
            pltpu.VMEM((tm, d), jnp.float32),
            pltpu.VMEM((tm, d), jnp.float32),
            pltpu.VMEM((2, d, tf), jnp.float32),
            pltpu.VMEM((2, d, tf), jnp.float32),
            pltpu.VMEM((2, tf, d), jnp.float32),
            pltpu.SemaphoreType.DMA((3,)),
            pltpu.SemaphoreType.DMA((3, 2)),
        ],
        input_output_aliases={8: 0},
        compiler_params=_cparams(("arbitrary",)),
        name="expert_ffn",
    )(idx3, idx3, idx3, gate3, wg, wu, wd, h2, x1)


def _pick(n, candidates):
    for c in candidates:
        if n % c == 0:
            return c
    raise ValueError(f"no tile among {candidates} divides {n}")


def _layer(x_p, x_s, seq_p, seq_s, rel_table, norm_mix_g, w_in, q_norm_a, k_norm_a, q_norm_b,
           k_norm_b, sink_b, w_proj_a, w_proj_b, w_out, norm_ffn_g, w_router, w_gate_e, w_up_e,
           w_down_e):
    np_tok, d = x_p.shape
    ns_tok = x_s.shape[0]
    ntok = np_tok + ns_tok
    bf = jnp.bfloat16
    tm_in = _pick(math.gcd(np_tok, ns_tok), (1024, 512, 256))

    def group_cols(g):
        return [w_in[:, s * A_QKV + g * GROUP_COLS: s * A_QKV + (g + 1) * GROUP_COLS] for s in range(3)]

    w_nat = jnp.concatenate(group_cols(0) + [w_in[:, 3 * A_QKV:]], axis=1).astype(bf)
    z = _inproj(x_p, x_s, norm_mix_g, w_nat, tm_in, _pick(w_nat.shape[1], (1024, 512)))

    heads_a = tuple((h * HEAD_DIM, GROUP_COLS + h * HEAD_DIM, 2 * GROUP_COLS + h * HEAD_DIM)
                    for h in range(A_HEADS))
    heads_b = tuple((h * HEAD_DIM, B_Q + (h // B_GROUP) * HEAD_DIM,
                     B_Q + B_KV + (h // B_GROUP) * HEAD_DIM) for h in range(B_Q_HEADS))
    oas, lses = [], []
    for g in range(N_DIL_GROUPS):
        dil = DIL_RATES[g]
        radius = (DIL_WINDOWS[g] // 2) // dil
        tq = min(ATTN_TQ, seq_p // dil, seq_s // dil)
        bias = _bias_tile(rel_table[:, g * A_HEADS:(g + 1) * A_HEADS], dil, radius,
                          min(ATTN_SUB, tq))
        if dil == 1:
            zv = z
        else:
            w_g = jnp.concatenate(group_cols(g), axis=1).astype(bf)
            zv = _inproj(x_p, x_s, norm_mix_g, w_g, tm_in, w_g.shape[1], dil=dil)
        o, lse = _banded_attention(
            zv, dil=dil, col0=0, radius=radius, tq=tq, heads=heads_a,
            res_per_step=ATTN_RES_PER_STEP if (dil > 1 and tq <= ATTN_SUB) else 1, bias=bias,
            q_w=q_norm_a, k_w=k_norm_a, sink=None, with_lse=True, np_tok=np_tok, seq_p=seq_p,
            seq_s=seq_s)
        oas.append(o)
        lses.append(lse)
    tq_b = min(ATTN_TQ, seq_p, seq_s)
    bias_b = _bias_tile(rel_table[:, N_DIL_GROUPS * A_HEADS:], 1, B_RADIUS, min(ATTN_SUB_B, tq_b))
    ob = _banded_attention(
        z, dil=1, col0=QKV_COLS, radius=B_RADIUS, tq=tq_b, heads=heads_b, res_per_step=1,
        bias=bias_b, q_w=q_norm_b, k_w=k_norm_b, sink=sink_b, with_lse=False, np_tok=np_tok,
        seq_p=seq_p, seq_s=seq_s)

    x1_p, x1_s, h2_p, h2_s, aff = _merge(
        x_p, x_s, oas, lses, ob, z, w_proj_a.astype(bf), w_proj_b.astype(bf), w_out.astype(bf),
        norm_ffn_g, w_router.T.astype(bf), _pick(math.gcd(np_tok, ns_tok), (256,)))

    n_exp = w_router.shape[1]
    tf = _pick(w_gate_e.shape[2], (256, 128))
    outs = []
    for x1, h2, aff_set in ((x1_p, h2_p, aff[:np_tok // LANES]), (x1_s, h2_s, aff[np_tok // LANES:])):
        cap = max(1, EC_CAPACITY * x1.shape[0] // n_exp)
        idx, gate = _route_set(aff_set, cap, 0)
        outs.append(_moe(idx, gate, h2, x1, w_gate_e, w_up_e, w_down_e, _pick(cap, (1024,)), tf))
    return outs


def kernel(x_prompt, x_sample, rel_table, norm_mix_g, w_in, q_norm_a, k_norm_a, q_norm_b, k_norm_b,
           sink_b, w_proj_a, w_proj_b, w_out, norm_ffn_g, w_router, w_gate_e, w_up_e, w_down_e):
    bp, sp, d = x_prompt.shape
    bs, ss, _ = x_sample.shape
    x_p = x_prompt.reshape(bp * sp, d)
    x_s = x_sample.reshape(bs * ss, d)
    for l in range(norm_mix_g.shape[0]):
        x_p, x_s = _layer(x_p, x_s, sp, ss, rel_table, norm_mix_g[l], w_in[l], q_norm_a[l],
                          k_norm_a[l], q_norm_b[l], k_norm_b[l], sink_b[l], w_proj_a[l],
                          w_proj_b[l], w_out[l], norm_ffn_g[l], w_router[l], w_gate_e[l],
                          w_up_e[l], w_down_e[l])
    return x_p.reshape(bp, sp, d), x_s.reshape(bs, ss, d)
```

```python
import functools
import math

import jax
import jax.numpy as jnp
from jax import lax
from jax.experimental import pallas as pl
from jax.experimental.pallas import tpu as pltpu

HEAD_DIM = 128
DIL_WINDOWS = (128, 512, 2048)
DIL_RATES = (1, 4, 16)
N_DIL_GROUPS = 3
A_HEADS = 4
B_Q_HEADS = 8
B_KV_HEADS = 2
B_GROUP = B_Q_HEADS // B_KV_HEADS
B_RADIUS = 128
REL_BUCKETS = 32
REL_MAX_DIST = 1024
EC_CAPACITY = 2
NORM_EPS = 1e-6
NEG_INF = -1e30

A_QKV = N_DIL_GROUPS * A_HEADS * HEAD_DIM
A_OUT = A_HEADS * HEAD_DIM
B_Q = B_Q_HEADS * HEAD_DIM
B_KV = B_KV_HEADS * HEAD_DIM

LANES = 128
GROUP_COLS = A_HEADS * HEAD_DIM
VMEM_LIMIT = 60 * 1024 * 1024
INPROJ_CHUNK = 256
QKV_COLS = 3 * GROUP_COLS
ATTN_TQ = 512
ATTN_SUB = 128
ATTN_SUB_B = 128
ATTN_RES_PER_STEP = 4
COMPACT_CHUNKS = 8
ROW_DMA_UNROLL = 8


def _cparams(sem, vmem=VMEM_LIMIT):
    return pltpu.CompilerParams(dimension_semantics=sem, vmem_limit_bytes=vmem)


def _inproj_kernel(xp_ref, xs_ref, g_ref, w_ref, z_ref, h_scr, *, tm, n_p):
    def normalise(x_ref):
        x = x_ref[...]
        ms = jnp.mean(x * x, axis=-1, keepdims=True)
        h_scr[...] = (x * lax.rsqrt(ms + NORM_EPS) * g_ref[...]).astype(jnp.bfloat16)

    first_col = pl.program_id(1) == 0
    is_prompt = pl.program_id(0) < n_p
    pl.when(jnp.logical_and(first_col, is_prompt))(lambda: normalise(xp_ref))
    pl.when(jnp.logical_and(first_col, jnp.logical_not(is_prompt)))(lambda: normalise(xs_ref))

    z_ref[...] = jnp.dot(h_scr[...], w_ref[...],
                         preferred_element_type=jnp.float32).astype(jnp.bfloat16)


def _inproj_strided_kernel(xp_ref, xs_ref, g_ref, w_ref, z_ref, zs, *, dil, tm, n_p):
    tn = w_ref.shape[1]
    chunk = INPROJ_CHUNK
    rows = chunk // dil

    def run(x_ref):
        for ci in range(tm // chunk):
            x = x_ref[ci * chunk:(ci + 1) * chunk, :]
            ms = jnp.mean(x * x, axis=-1, keepdims=True)
            h = (x * lax.rsqrt(ms + NORM_EPS) * g_ref[...]).astype(jnp.bfloat16)
            z = jnp.dot(h, w_ref[...], preferred_element_type=jnp.float32)
            slab = zs.at[ci % 2]
            for c in range(tn // LANES):
                slab[c] = z[:, c * LANES:(c + 1) * LANES]
            for r in range(dil):
                for c in range(tn // LANES):
                    z_ref[ci * rows:(ci + 1) * rows, r * tn + c * LANES:r * tn + (c + 1) * LANES] = (
                        slab[c, pl.ds(r, rows, stride=dil), :].astype(jnp.bfloat16))

    is_prompt = pl.program_id(0) < n_p
    pl.when(is_prompt)(lambda: run(xp_ref))
    pl.when(jnp.logical_not(is_prompt))(lambda: run(xs_ref))


def _inproj(x_p, x_s, g, w_bf16, tm, tn, dil=1):
    d = x_p.shape[1]
    n_p = x_p.shape[0] // tm
    n = x_p.shape[0] + x_s.shape[0]
    cols = w_bf16.shape[1]
    assert dil == 1 or tn == cols
    if dil == 1:
        kern = functools.partial(_inproj_kernel, tm=tm, n_p=n_p)
        scratch = [pltpu.VMEM((tm, d), jnp.bfloat16)]
    else:
        kern = functools.partial(_inproj_strided_kernel, dil=dil, tm=tm, n_p=n_p)
        scratch = [pltpu.VMEM((2, tn // LANES, INPROJ_CHUNK, LANES), jnp.float32)]
    return pl.pallas_call(
        kern,
        grid=(n // tm, cols // tn),
        in_specs=[
            pl.BlockSpec((tm, d), lambda i, j: (jnp.minimum(i, n_p - 1), 0)),
            pl.BlockSpec((tm, d), lambda i, j: (jnp.maximum(i - n_p, 0), 0)),
            pl.BlockSpec((1, d), lambda i, j: (0, 0)),
            pl.BlockSpec((d, tn), lambda i, j: (0, j)),
        ],
        out_specs=pl.BlockSpec((tm // dil, dil * tn), lambda i, j: (i, j)),
        out_shape=jax.ShapeDtypeStruct((n // dil, dil * cols), jnp.bfloat16),
        scratch_shapes=scratch,
        compiler_params=_cparams(("parallel", "arbitrary")),
        name=f"inproj_d{dil}",
    )(x_p, x_s, g.reshape(1, d), w_bf16)


def _t5_bucket(rel):
    half = REL_BUCKETS // 2
    max_exact = half // 2
    n = jnp.abs(rel)
    base = jnp.where(rel > 0, half, 0)
    nf = jnp.maximum(n, 1).astype(jnp.float32)
    large = max_exact + (jnp.log(nf / max_exact) / math.log(REL_MAX_DIST / max_exact)
                         * (half - max_exact)).astype(jnp.int32)
    large = jnp.minimum(large, half - 1)
    return base + jnp.where(n < max_exact, n, large)


def _bias_tile(table_cols, dil, radius, tq):
    tk = tq + 2 * radius
    n_heads = table_cols.shape[1]
    rel = jnp.arange(-radius, radius + 1)
    vals = table_cols[_t5_bucket(rel * dil)].astype(jnp.float32).T
    period = tq + tk
    w = jnp.full((n_heads, period), NEG_INF, jnp.float32).at[:, :2 * radius + 1].set(vals)
    flat = jnp.tile(w, (1, tq))[:, :tq * (period - 1)]
    return flat.reshape(n_heads, tq, period - 1)[:, :, :tk]


def _head_norm(x, w):
    xf = x.astype(jnp.float32)
    ms = jnp.mean(xf * xf, axis=-1, keepdims=True)
    return xf * lax.rsqrt(ms + NORM_EPS) * w


def _attn_kernel(*refs, tq, sub, radius, heads, res_per_step, with_sink, with_lse, np_rows,
                 len_p, len_s):
    cur_ref, prev_ref, next_ref, bias_ref, qw_ref, kw_ref = refs[:6]
    pos = 6
    sink_ref = None
    if with_sink:
        sink_ref = refs[pos]
        pos += 1
    o_ref = refs[pos]
    lse_ref = refs[pos + 1] if with_lse else None

    win = sub + 2 * radius
    n_heads = len(heads)
    q0 = pl.program_id(1) * tq
    lo_p = (q0 // len_p) * len_p
    lo_s = np_rows + ((q0 - np_rows) // len_s) * len_s
    in_p = q0 < np_rows
    lo = jnp.where(in_p, lo_p, lo_s)
    hi = lo + jnp.where(in_p, len_p, len_s)
    key_iota = lax.broadcasted_iota(jnp.int32, (1, win), 1)
    lane = lax.broadcasted_iota(jnp.int32, (sub, LANES), 1)
    scale = HEAD_DIM ** -0.5
    qw = qw_ref[...]
    kw = kw_ref[...]

    def window(col):
        cs = slice(col, col + HEAD_DIM)
        return jnp.concatenate([prev_ref[:, cs], cur_ref[:, cs], next_ref[:, cs]], axis=0)

    kv_groups = {}
    for h, (_, k_off, v_off) in enumerate(heads):
        kv_groups.setdefault((k_off, v_off), []).append(h)

    def stack(pieces):
        return pieces[0] if len(pieces) == 1 else jnp.concatenate(pieces, axis=0)

    for rr in range(res_per_step):
        base = rr * QKV_COLS
        lse_tiles = [jnp.zeros((sub, LANES), jnp.float32) for _ in range(tq // sub)]
        for (k_off, v_off), hs in kv_groups.items():
            kh = _head_norm(window(base + k_off), kw).astype(jnp.bfloat16)
            vh = window(base + v_off)
            qs = [(_head_norm(cur_ref[:, base + heads[h][0]:base + heads[h][0] + HEAD_DIM], qw)
                   * scale).astype(jnp.bfloat16) for h in hs]
            bias = stack([bias_ref[h] for h in hs])
            if with_sink:
                sink = stack([sink_ref[h] for h in hs])
                is_sink = sink > NEG_INF
            for a in range(tq // sub):
                kpos = q0 + a * sub - radius + key_iota
                valid = (kpos >= lo) & (kpos < hi)
                s = lax.dot_general(stack([q[a * sub:(a + 1) * sub] for q in qs]),
                                    kh[a * sub:a * sub + win], (((1,), (1,)), ((), ())),
                                    preferred_element_type=jnp.float32)
                s = jnp.where(valid, s + bias, NEG_INF)
                if with_sink:
                    s = jnp.maximum(s, sink)
                m = jnp.max(s, axis=-1, keepdims=True)
                p = jnp.exp(s - m)
                l = jnp.sum(p, axis=-1, keepdims=True)
                if with_sink:
                    p = jnp.where(is_sink, 0.0, p)
                o = jnp.dot(p.astype(jnp.bfloat16), vh[a * sub:a * sub + win],
                            preferred_element_type=jnp.float32) / l
                lse = m + jnp.log(l)
                for gi, h in enumerate(hs):
                    oc = (rr * n_heads + h) * HEAD_DIM
                    o_ref[a * sub:(a + 1) * sub, oc:oc + HEAD_DIM] = (
                        o[gi * sub:(gi + 1) * sub].astype(o_ref.dtype))
                    if with_lse:
                        lse_tiles[a] = jnp.where(lane == h, lse[gi * sub:(gi + 1) * sub],
                                                 lse_tiles[a])
        if with_lse:
            for a in range(tq // sub):
                lse_ref[a * sub:(a + 1) * sub, rr * LANES:(rr + 1) * LANES] = lse_tiles[a]


def _banded_attention(zv, *, dil, col0, radius, tq, heads, res_per_step, bias, q_w, k_w, sink,
                      with_lse, np_tok, seq_p, seq_s):
    rows = zv.shape[0]
    ntok = rows * dil
    n_heads = len(heads)
    assert col0 % QKV_COLS == 0 and dil % res_per_step == 0
    assert dil == 1 or zv.shape[1] == dil * QKV_COLS
    blk0 = col0 // QKV_COLS
    nq = rows // tq
    hb = tq // radius
    n_halo = rows // radius
    sub = bias.shape[1]
    wblk = res_per_step * QKV_COLS
    const2 = lambda r, i: (0, 0)
    const3 = lambda r, i: (0, 0, 0)
    in_specs = [
        pl.BlockSpec((tq, wblk), lambda r, i: (i, blk0 + r)),
        pl.BlockSpec((radius, wblk), lambda r, i: (jnp.maximum(i * hb - 1, 0), blk0 + r)),
        pl.BlockSpec((radius, wblk), lambda r, i: (jnp.minimum((i + 1) * hb, n_halo - 1), blk0 + r)),
        pl.BlockSpec((n_heads, sub, sub + 2 * radius), const3),
        pl.BlockSpec((1, HEAD_DIM), const2),
        pl.BlockSpec((1, HEAD_DIM), const2),
    ]
    args = [zv, zv, zv, bias, q_w.reshape(1, HEAD_DIM), k_w.reshape(1, HEAD_DIM)]
    if sink is not None:
        win = sub + 2 * radius
        row = jnp.arange(sub)
        col = jnp.where(row + 2 * radius + 1 < win, row + 2 * radius + 1, row - 1)
        at_col = jnp.arange(win)[None, :] == col[:, None]
        in_specs.append(pl.BlockSpec((n_heads, sub, win), const3))
        args.append(jnp.where(at_col[None], sink.astype(jnp.float32)[:, None, None], NEG_INF))
    omap = lambda r, i: (i, r)
    out_specs = [pl.BlockSpec((tq, res_per_step * n_heads * HEAD_DIM), omap)]
    out_shape = [jax.ShapeDtypeStruct((rows, dil * n_heads * HEAD_DIM), jnp.bfloat16)]
    if with_lse:
        out_specs.append(pl.BlockSpec((tq, res_per_step * LANES), omap))
        out_shape.append(jax.ShapeDtypeStruct((rows, dil * LANES), jnp.float32))
    kern = functools.partial(
        _attn_kernel, tq=tq, sub=sub, radius=radius, heads=heads, res_per_step=res_per_step,
        with_sink=sink is not None, with_lse=with_lse, np_rows=np_tok // dil,
        len_p=seq_p // dil, len_s=seq_s // dil)
    outs = pl.pallas_call(
        kern, grid=(dil // res_per_step, nq), in_specs=in_specs, out_specs=out_specs,
        out_shape=out_shape, compiler_params=_cparams(("parallel", "arbitrary")),
        name=f"band_attn_d{dil}_r{radius}",
    )(*args)
    return tuple(outs) if with_lse else outs[0]


def _merge_kernel(xp_ref, xs_ref, oa0_ref, oa1_ref, oa2_ref, l0_ref, l1_ref, l2_ref, ob_ref, ga0_ref,
                  ga1_ref, gb0_ref, gb1_ref, wpa_ref, wpb_ref, wo_ref, g2_ref, wr_ref, x1p_ref,
                  x1s_ref, h2p_ref, h2s_ref, aff_ref, oa_scr, l_scr, merged_scr, *, tm, n_p):
    step = pl.program_id(0)
    is_prompt = jnp.maximum(step - 1, 0) < n_p

    @pl.when(step == 0)
    def _():
        merged_scr[...] = jnp.zeros_like(merged_scr)

    delta = jnp.dot(merged_scr[...], wo_ref[...], preferred_element_type=jnp.float32)
    ga = jnp.concatenate([ga0_ref[...], ga1_ref[...]], axis=1)
    gb = jnp.concatenate([gb0_ref[...], gb1_ref[...]], axis=1)

    def natural_order(g, o_ref, l_ref):
        dil = DIL_RATES[g]
        if dil == 1:
            return ([o_ref[:, h * HEAD_DIM:(h + 1) * HEAD_DIM].astype(jnp.float32)
                     for h in range(A_HEADS)], l_ref[...])
        rows = tm // dil
        for r in range(dil):
            for h in range(A_HEADS):
                c0 = r * A_OUT + h * HEAD_DIM
                oa_scr[g, h, pl.ds(r, rows, stride=dil), :] = (
                    o_ref[:, c0:c0 + HEAD_DIM].astype(jnp.float32))
            l_scr[g, pl.ds(r, rows, stride=dil), :] = l_ref[:, r * LANES:(r + 1) * LANES]
        return [oa_scr[g, h] for h in range(A_HEADS)], l_scr[g]

    (o0, l0), (o1, l1), (o2, l2) = (natural_order(0, oa0_ref, l0_ref),
                                    natural_order(1, oa1_ref, l1_ref),
                                    natural_order(2, oa2_ref, l2_ref))
    mx = jnp.maximum(jnp.maximum(l0, l1), l2)
    e0, e1, e2 = jnp.exp(l0 - mx), jnp.exp(l1 - mx), jnp.exp(l2 - mx)
    den = e0 + e1 + e2
    w0, w1, w2 = e0 / den, e1 / den, e2 / den
    parts = [w0[:, h:h + 1] * o0[h] + w1[:, h:h + 1] * o1[h] + w2[:, h:h + 1] * o2[h]
             for h in range(A_HEADS)]
    o_a = jnp.concatenate(parts, axis=1).astype(jnp.bfloat16)
    pa = jnp.dot(o_a, wpa_ref[...], preferred_element_type=jnp.float32)
    pb = jnp.dot(ob_ref[...], wpb_ref[...], preferred_element_type=jnp.float32)
    merged = (jax.nn.sigmoid(ga.astype(jnp.float32)) * pa
              + jax.nn.sigmoid(gb.astype(jnp.float32)) * pb)
    merged_scr[...] = merged.astype(jnp.bfloat16)

    def finish(x_ref, x1_ref, h2_ref):
        x1 = x_ref[...] + delta
        x1_ref[...] = x1
        ms = jnp.mean(x1 * x1, axis=-1, keepdims=True)
        h2 = x1 * lax.rsqrt(ms + NORM_EPS) * g2_ref[...]
        h2_ref[...] = h2
        logits = lax.dot_general(wr_ref[...], h2.astype(jnp.bfloat16), (((1,), (1,)), ((), ())),
                                 preferred_element_type=jnp.float32)
        mx2 = jnp.max(logits, axis=0, keepdims=True)
        ex = jnp.exp(logits - mx2)
        aff = ex / jnp.sum(ex, axis=0, keepdims=True)
        for j in range(tm // LANES):
            aff_ref[j] = aff[:, j * LANES:(j + 1) * LANES]

    pl.when(is_prompt)(lambda: finish(xp_ref, x1p_ref, h2p_ref))
    pl.when(jnp.logical_not(is_prompt))(lambda: finish(xs_ref, x1s_ref, h2s_ref))


def _merge(x_p, x_s, oas, lses, ob, z, wpa, wpb, wo, g2, wr_t, tm):
    d = x_p.shape[1]
    n_p = x_p.shape[0] // tm
    n = x_p.shape[0] + x_s.shape[0]
    n_exp = wr_t.shape[0]
    half = d // 2
    assert (2 * QKV_COLS) % half == 0
    ga_blk = 2 * QKV_COLS // half
    n_tiles = n // tm
    formed = lambda i: jnp.minimum(i, n_tiles - 1)
    finished = lambda i: jnp.maximum(i - 1, 0)
    gate = lambda k: pl.BlockSpec((tm, half), lambda i: (formed(i), ga_blk + k))
    row = lambda i: (formed(i), 0)
    const = lambda i: (0, 0)
    row_p = lambda i: (jnp.minimum(finished(i), n_p - 1), 0)
    row_s = lambda i: (jnp.maximum(finished(i) - n_p, 0), 0)
    in_specs = [
        pl.BlockSpec((tm, d), row_p),
        pl.BlockSpec((tm, d), row_s),
        *[pl.BlockSpec((tm // dl, dl * A_OUT), row) for dl in DIL_RATES],
        *[pl.BlockSpec((tm // dl, dl * LANES), row) for dl in DIL_RATES],
        pl.BlockSpec((tm, B_Q), row),
        gate(0), gate(1), gate(2), gate(3),
        pl.BlockSpec((A_OUT, d), const),
        pl.BlockSpec((B_Q, d), const),
        pl.BlockSpec((d, d), const),
        pl.BlockSpec((1, d), const),
        pl.BlockSpec((n_exp, d), const),
    ]
    out_specs = [
        pl.BlockSpec((tm, d), row_p),
        pl.BlockSpec((tm, d), row_s),
        pl.BlockSpec((tm, d), row_p),
        pl.BlockSpec((tm, d), row_s),
        pl.BlockSpec((tm // LANES, n_exp, LANES), lambda i: (finished(i), 0, 0)),
    ]
    out_shape = [
        jax.ShapeDtypeStruct(x_p.shape, jnp.float32),
        jax.ShapeDtypeStruct(x_s.shape, jnp.float32),
        jax.ShapeDtypeStruct(x_p.shape, jnp.float32),
        jax.ShapeDtypeStruct(x_s.shape, jnp.float32),
        jax.ShapeDtypeStruct((n // LANES, n_exp, LANES), jnp.float32),
    ]
    return pl.pallas_call(
        functools.partial(_merge_kernel, tm=tm, n_p=n_p),
        grid=(n_tiles + 1,), in_specs=in_specs, out_specs=out_specs, out_shape=out_shape,
        scratch_shapes=[pltpu.VMEM((N_DIL_GROUPS, A_HEADS, tm, LANES), jnp.float32),
                        pltpu.VMEM((N_DIL_GROUPS, tm, LANES), jnp.float32),
                        pltpu.VMEM((tm, d), jnp.bfloat16)],
        compiler_params=_cparams(("arbitrary",)),
        name="merge_proj_router",
    )(x_p, x_s, oas[0], oas[1], oas[2], lses[0], lses[1], lses[2], ob, z, z, z, z, wpa, wpb, wo,
      g2.reshape(1, d), wr_t)


def _prefix_counts(flag_f32, tri, tot_scr, off_scr, nc):
    n_exp = flag_f32.shape[1]
    incl = jnp.dot(flag_f32.astype(jnp.bfloat16).reshape(nc * n_exp, LANES), tri,
                   preferred_element_type=jnp.float32).reshape(nc, n_exp, LANES)
    tot_scr[...] = jnp.broadcast_to(incl[:, :, LANES - 1:LANES], (nc, n_exp, LANES))

    def body(c, run):
        off_scr[c] = run
        return run + tot_scr[c]

    lax.fori_loop(0, nc, body, jnp.zeros((n_exp, LANES), jnp.float32))
    return off_scr[...] + incl - flag_f32


def _route_kernel(aff_ref, pos_ref, off_ref, tot_scr, off_scr, *, cap, nc):
    n_exp = aff_ref.shape[1]
    capf = jnp.float32(cap)

    def count(mask):
        c = jnp.sum(jnp.where(mask, 1.0, 0.0), axis=0)
        return jnp.sum(c, axis=-1, keepdims=True)

    def bit_body(k, t):
        cand = t | jnp.left_shift(jnp.int32(1), 30 - k)
        bits = pltpu.bitcast(aff_ref[...], jnp.int32)
        return jnp.where(count(bits >= cand[None]) >= capf, cand, t)

    t = lax.fori_loop(0, 31, bit_body, jnp.zeros((n_exp, 1), jnp.int32))
    bits = pltpu.bitcast(aff_ref[...], jnp.int32)
    gt = bits > t[None]
    eq = bits == t[None]
    need = capf - count(gt)
    rows = lax.broadcasted_iota(jnp.int32, (LANES, LANES), 0)
    cols = lax.broadcasted_iota(jnp.int32, (LANES, LANES), 1)
    tri = jnp.where(rows <= cols, 1.0, 0.0).astype(jnp.bfloat16)
    eq_f = jnp.where(eq, 1.0, 0.0)
    tie_rank = _prefix_counts(eq_f, tri, tot_scr, off_scr, nc)
    sel = gt | (eq & (tie_rank < need[None]))
    sel_f = jnp.where(sel, 1.0, 0.0)
    slot = _prefix_counts(sel_f, tri, tot_scr, off_scr, nc)
    pos_ref[...] = jnp.where(sel, slot, -1.0).astype(jnp.int32)
    off_ref[...] = off_scr[...].astype(jnp.int32)


def _route(aff, cap):
    nc, n_exp, _ = aff.shape
    full = pl.BlockSpec((nc, n_exp, LANES), lambda: (0, 0, 0))
    return pl.pallas_call(
        functools.partial(_route_kernel, cap=cap, nc=nc),
        in_specs=[full], out_specs=[full, full],
        out_shape=[jax.ShapeDtypeStruct((nc, n_exp, LANES), jnp.int32)] * 2,
        scratch_shapes=[pltpu.VMEM((nc, n_exp, LANES), jnp.float32)] * 2,
        compiler_params=pltpu.CompilerParams(vmem_limit_bytes=VMEM_LIMIT),
        name="route_select",
    )(aff)


def _split3(x):
    a = x.astype(jnp.bfloat16).astype(jnp.float32)
    r = x - a
    b = r.astype(jnp.bfloat16).astype(jnp.float32)
    c = r - b
    return a, b, c


def _compact_kernel(off_smem, pos_ref, aff_ref, idx_ref, gate_ref, acc_scr, *, nc, n_blk, tok_base):
    e = pl.program_id(0)
    stride = nc + 1
    sub = lax.broadcasted_iota(jnp.int32, (LANES, LANES), 0)
    sub16 = lax.broadcasted_iota(jnp.int32, (16, LANES), 0)
    lane = lax.broadcasted_iota(jnp.int32, (1, LANES), 1)

    n_grp = nc // COMPACT_CHUNKS

    def chunk_off(c):
        return off_smem[e * stride + jnp.minimum(c, nc)]

    def block_body(sb, g_first):
        lo_slot = sb * LANES

        def skip_cond(g):
            return jnp.logical_and(g < n_grp, chunk_off((g + 1) * COMPACT_CHUNKS) <= lo_slot)

        g_first = lax.while_loop(skip_cond, lambda g: g + 1, g_first)
        acc_scr[...] = jnp.zeros_like(acc_scr)

        def take_cond(g):
            return jnp.logical_and(g < n_grp, chunk_off(g * COMPACT_CHUNKS) < lo_slot + LANES)

        def take(g):
            total = jnp.zeros((16, LANES), jnp.float32)
            for k in range(COMPACT_CHUNKS):
                c = g * COMPACT_CHUNKS + k
                rel = pos_ref[c, pl.ds(e, 1), :] - lo_slot
                onehot = jnp.where(sub == rel, 1.0, 0.0).astype(jnp.bfloat16)
                tok = tok_base + c * LANES + lane
                g1, g2, g3 = _split3(aff_ref[c, pl.ds(e, 1), :])
                lhs = jnp.where(sub16 == 0, (tok >> 8).astype(jnp.float32),
                      jnp.where(sub16 == 1, (tok & 255).astype(jnp.float32),
                      jnp.where(sub16 == 2, g1,
                      jnp.where(sub16 == 3, g2,
                      jnp.where(sub16 == 4, g3, 0.0))))).astype(jnp.bfloat16)
                total = total + lax.dot_general(lhs, onehot, (((1,), (1,)), ((), ())),
                                                preferred_element_type=jnp.float32)
            acc_scr[...] += total
            return g + 1

        lax.while_loop(take_cond, take, g_first)
        acc = acc_scr[...]
        idx_ref[0, pl.ds(sb, 1), :] = (acc[0:1] * 256.0 + acc[1:2]).astype(jnp.int32)
        gate_ref[0, pl.ds(sb, 1), :] = (acc[2:3] + acc[3:4]) + acc[4:5]
        return g_first

    lax.fori_loop(0, n_blk, block_body, jnp.int32(0))


def _compact(offs_flat, pos, aff, cap, tok_base):
    nc, n_exp, _ = pos.shape
    n_blk = cap // LANES
    full = pl.BlockSpec((nc, n_exp, LANES), lambda e, off: (0, 0, 0))
    out = pl.BlockSpec((1, n_blk, LANES), lambda e, off: (e, 0, 0))
    return pl.pallas_call(
        functools.partial(_compact_kernel, nc=nc, n_blk=n_blk, tok_base=tok_base),
        grid_spec=pltpu.PrefetchScalarGridSpec(
            num_scalar_prefetch=1, grid=(n_exp,), in_specs=[full, full], out_specs=[out, out],
            scratch_shapes=[pltpu.VMEM((16, LANES), jnp.float32)]),
        out_shape=[jax.ShapeDtypeStruct((n_exp, n_blk, LANES), jnp.int32),
                   jax.ShapeDtypeStruct((n_exp, n_blk, LANES), jnp.float32)],
        compiler_params=_cparams(("arbitrary",)),
        name="route_compact",
    )(offs_flat, pos, aff)


def _route_set(aff, cap, tok_base):
    nc, n_exp, _ = aff.shape
    pos, off = _route(aff, cap)
    offs = jnp.concatenate([off[:, :, 0].T, jnp.full((n_exp, 1), cap, jnp.int32)], axis=1)
    idx, gate = _compact(offs.reshape(-1), pos, aff, cap, tok_base)
    return idx.reshape(n_exp, cap), gate.reshape(n_exp, cap)


def _moe_schedule(tm, n_f):
    phases = ((4, "scatter", tm // 4), (2, "next", tm // 4), (8, "residual", tm // 8),
              (4, "next", tm // 8), (n_f - 18, None, 0))
    assert n_f >= 18 and n_f % 2 == 0 and tm % 8 == 0
    return phases


def _moe_kernel(idx_prev, idx_cur, idx_nxt, gate_ref, wg_hbm, wu_hbm, wd_hbm, h2_hbm, x1_hbm,
                out_hbm, xe32, xe16, acc, orow, wg_buf, wu_buf, wd_buf, sems, wsems,
                *, tm, tf, nt, n_tiles, n_f):
    del x1_hbm
    n = pl.program_id(0)
    sem_x, sem_g, sem_s = sems.at[0], sems.at[1], sems.at[2]
    last_step = n_tiles * n_f - 1

    def weight_copies(step, slot):
        step = jnp.minimum(step, last_step)
        e = step // (nt * n_f)
        col = pl.multiple_of((step % n_f) * tf, tf)
        return (
            pltpu.make_async_copy(wg_hbm.at[e, :, pl.ds(col, tf)], wg_buf.at[slot], wsems.at[0, slot]),
            pltpu.make_async_copy(wu_hbm.at[e, :, pl.ds(col, tf)], wu_buf.at[slot], wsems.at[1, slot]),
            pltpu.make_async_copy(wd_hbm.at[e, pl.ds(col, tf), :], wd_buf.at[slot], wsems.at[2, slot]),
        )

    def xe_row(idx_smem, s):
        return pltpu.make_async_copy(h2_hbm.at[pl.ds(idx_smem[0, 0, s], 1)], xe32.at[pl.ds(s, 1)], sem_x)

    def residual_row(s):
        return pltpu.make_async_copy(out_hbm.at[pl.ds(idx_cur[0, 0, s], 1)], orow.at[pl.ds(s, 1)], sem_g)

    def scatter_row(idx_smem, s):
        return pltpu.make_async_copy(orow.at[pl.ds(s, 1)], out_hbm.at[pl.ds(idx_smem[0, 0, s], 1)], sem_s)

    def wait_xe():
        pltpu.make_async_copy(h2_hbm.at[pl.ds(0, tm)], xe32, sem_x).wait()

    def wait_scatter():
        pltpu.make_async_copy(orow, out_hbm.at[pl.ds(0, tm)], sem_s).wait()

    def ffn_step(f, slot):
        for c in weight_copies(n * n_f + f, slot):
            c.wait()
        for c in weight_copies(n * n_f + f + 1, 1 - slot):
            c.start()
        x = xe16[...]
        gp = jnp.dot(x, wg_buf[slot].astype(jnp.bfloat16), preferred_element_type=jnp.float32)
        up = jnp.dot(x, wu_buf[slot].astype(jnp.bfloat16), preferred_element_type=jnp.float32)
        hid = (gp * jax.nn.sigmoid(gp) * up).astype(jnp.bfloat16)
        acc[...] += jnp.dot(hid, wd_buf[slot].astype(jnp.bfloat16),
                            preferred_element_type=jnp.float32)

    def run_steps(f_lo, count, row_work):
        def body(p, _):
            for slot in (0, 1):
                ffn_step(f_lo + 2 * p + slot, slot)
                row_work(2 * p + slot)
            return 0
        lax.fori_loop(0, count // 2, body, 0)

    @pl.when(n == 0)
    def _():
        def body(s, _):
            xe_row(idx_cur, s).start()
            return 0
        lax.fori_loop(0, tm, body, 0, unroll=ROW_DMA_UNROLL)
        for c in weight_copies(0, 0):
            c.start()

    wait_xe()
    xe16[...] = xe32[...].astype(jnp.bfloat16)
    acc[...] = jnp.zeros_like(acc)

    f_lo = 0
    next_base = 0
    scatter_pending = False
    for steps, kind, rows in _moe_schedule(tm, n_f):
        if steps == 0:
            continue
        if kind == "scatter":
            def scatter_prev(k, rows=rows):
                for r in range(rows):
                    scatter_row(idx_prev, k * rows + r).start()

            pl.when(n > 0)(functools.partial(run_steps, f_lo, steps, scatter_prev))
            pl.when(n == 0)(functools.partial(run_steps, f_lo, steps, lambda k: None))
            scatter_pending = True
        elif kind == "next":
            def gather_next(k, rows=rows, base=next_base):
                for r in range(rows):
                    xe_row(idx_nxt, base + k * rows + r).start()

            run_steps(f_lo, steps, gather_next)
            next_base += steps * rows
        elif kind == "residual":
            assert scatter_pending
            pl.when(n > 0)(wait_scatter)

            def gather_residual(k, rows=rows):
                for r in range(rows):
                    residual_row(k * rows + r).start()

            run_steps(f_lo, steps, gather_residual)
        else:
            run_steps(f_lo, steps, lambda k: None)
        f_lo += steps
    assert f_lo == n_f and next_base == tm

    pltpu.make_async_copy(out_hbm.at[pl.ds(0, tm)], orow, sem_g).wait()
    g_t = gate_ref[0].T
    for j in range(tm // LANES):
        rs = slice(j * LANES, (j + 1) * LANES)
        orow[rs, :] = orow[rs, :] + acc[rs, :] * g_t[:, j:j + 1]

    @pl.when(n == n_tiles - 1)
    def _():
        def body(s, _):
            scatter_row(idx_cur, s).start()
            return 0
        lax.fori_loop(0, tm, body, 0, unroll=ROW_DMA_UNROLL)
        wait_scatter()
        wait_xe()
        for c in weight_copies(last_step, 0):
            c.wait()


def _moe(idx, gate, h2, x1, wg, wu, wd, tm, tf):
    n_exp, slots = idx.shape
    ntok, d = x1.shape
    ff = wg.shape[2]
    nt = slots // tm
    n_f = ff // tf
    n_tiles = n_exp * nt
    _moe_schedule(tm, n_f)
    idx3 = idx.reshape(n_tiles, 1, tm)
    gate3 = gate.reshape(n_exp, slots // LANES, LANES)
    smem = pltpu.MemorySpace.SMEM
    hbm = pl.BlockSpec(memory_space=pl.ANY)
    in_specs = [
        pl.BlockSpec((1, 1, tm), lambda n: (jnp.maximum(n - 1, 0), 0, 0), memory_space=smem),
        pl.BlockSpec((1, 1, tm), lambda n: (n, 0, 0), memory_space=smem),
        pl.BlockSpec((1, 1, tm), lambda n: (jnp.minimum(n + 1, n_tiles - 1), 0, 0), memory_space=smem),
        pl.BlockSpec((1, tm // LANES, LANES), lambda n: (n // nt, n % nt, 0)),
        hbm, hbm, hbm, hbm, hbm,
    ]
    return pl.pallas_call(
        functools.partial(_moe_kernel, tm=tm, tf=tf, nt=nt, n_tiles=n_tiles, n_f=n_f),
        grid=(n_tiles,), in_specs=in_specs, out_specs=hbm,
        out_shape=jax.ShapeDtypeStruct((ntok, d), jnp.float32),
        scratch_shapes=[
            pltpu.VMEM((tm, d), jnp.float32),
            pltpu.VMEM((tm, d), jnp.bfloat16),
            pltpu.VMEM((tm, d), jnp.float32),
            pltpu.VMEM((tm, d), jnp.float32),
            pltpu.VMEM((2, d, tf), jnp.float32),
            pltpu.VMEM((2, d, tf), jnp.float32),
            pltpu.VMEM((2, tf, d), jnp.float32),
            pltpu.SemaphoreType.DMA((3,)),
            pltpu.SemaphoreType.DMA((3, 2)),
        ],
        input_output_aliases={8: 0},
        compiler_params=_cparams(("arbitrary",)),
        name="expert_ffn",
    )(idx3, idx3, idx3, gate3, wg, wu, wd, h2, x1)


def _pick(n, candidates):
    for c in candidates:
        if n % c == 0:
            return c
    raise ValueError(f"no tile among {candidates} divides {n}")


def _layer(x_p, x_s, seq_p, seq_s, rel_table, norm_mix_g, w_in, q_norm_a, k_norm_a, q_norm_b,
           k_norm_b, sink_b, w_proj_a, w_proj_b, w_out, norm_ffn_g, w_router, w_gate_e, w_up_e,
           w_down_e):
    np_tok, d = x_p.shape
    ns_tok = x_s.shape[0]
    ntok = np_tok + ns_tok
    bf = jnp.bfloat16
    tm_in = _pick(math.gcd(np_tok, ns_tok), (1024, 512, 256))

    def group_cols(g):
        return [w_in[:, s * A_QKV + g * GROUP_COLS: s * A_QKV + (g + 1) * GROUP_COLS] for s in range(3)]

    w_nat = jnp.concatenate(group_cols(0) + [w_in[:, 3 * A_QKV:]], axis=1).astype(bf)
    z = _inproj(x_p, x_s, norm_mix_g, w_nat, tm_in, _pick(w_nat.shape[1], (1024, 512)))

    heads_a = tuple((h * HEAD_DIM, GROUP_COLS + h * HEAD_DIM, 2 * GROUP_COLS + h * HEAD_DIM)
                    for h in range(A_HEADS))
    heads_b = tuple((h * HEAD_DIM, B_Q + (h // B_GROUP) * HEAD_DIM,
                     B_Q + B_KV + (h // B_GROUP) * HEAD_DIM) for h in range(B_Q_HEADS))
    oas, lses = [], []
    for g in range(N_DIL_GROUPS):
        dil = DIL_RATES[g]
        radius = (DIL_WINDOWS[g] // 2) // dil
        tq = min(ATTN_TQ, seq_p // dil, seq_s // dil)
        bias = _bias_tile(rel_table[:, g * A_HEADS:(g + 1) * A_HEADS], dil, radius,
                          min(ATTN_SUB, tq))
        if dil == 1:
            zv = z
        else:
            w_g = jnp.concatenate(group_cols(g), axis=1).astype(bf)
            zv = _inproj(x_p, x_s, norm_mix_g, w_g, tm_in, w_g.shape[1], dil=dil)
        o, lse = _banded_attention(
            zv, dil=dil, col0=0, radius=radius, tq=tq, heads=heads_a,
            res_per_step=ATTN_RES_PER_STEP if (dil > 1 and tq <= ATTN_SUB) else 1, bias=bias,
            q_w=q_norm_a, k_w=k_norm_a, sink=None, with_lse=True, np_tok=np_tok, seq_p=seq_p,
            seq_s=seq_s)
        oas.append(o)
        lses.append(lse)
    tq_b = min(ATTN_TQ, seq_p, seq_s)
    bias_b = _bias_tile(rel_table[:, N_DIL_GROUPS * A_HEADS:], 1, B_RADIUS, min(ATTN_SUB_B, tq_b))
    ob = _banded_attention(
        z, dil=1, col0=QKV_COLS, radius=B_RADIUS, tq=tq_b, heads=heads_b, res_per_step=1,
        bias=bias_b, q_w=q_norm_b, k_w=k_norm_b, sink=sink_b, with_lse=False, np_tok=np_tok,
        seq_p=seq_p, seq_s=seq_s)

    x1_p, x1_s, h2_p, h2_s, aff = _merge(
        x_p, x_s, oas, lses, ob, z, w_proj_a.astype(bf), w_proj_b.astype(bf), w_out.astype(bf),
        norm_ffn_g, w_router.T.astype(bf), _pick(math.gcd(np_tok, ns_tok), (256,)))

    n_exp = w_router.shape[1]
    tf = _pick(w_gate_e.shape[2], (256, 128))
    outs = []
    for x1, h2, aff_set in ((x1_p, h2_p, aff[:np_tok // LANES]), (x1_s, h2_s, aff[np_tok // LANES:])):
        cap = max(1, EC_CAPACITY * x1.shape[0] // n_exp)
        idx, gate = _route_set(aff_set, cap, 0)
        outs.append(_moe(idx, gate, h2, x1, w_gate_e, w_up_e, w_down_e, _pick(cap, (1024,)), tf))
    return outs


def kernel(x_prompt, x_sample, rel_table, norm_mix_g, w_in, q_norm_a, k_norm_a, q_norm_b, k_norm_b,
           sink_b, w_proj_a, w_proj_b, w_out, norm_ffn_g, w_router, w_gate_e, w_up_e, w_down_e):
    bp, sp, d = x_prompt.shape
    bs, ss, _ = x_sample.shape
    x_p = x_prompt.reshape(bp * sp, d)
    x_s = x_sample.reshape(bs * ss, d)
    for l in range(norm_mix_g.shape[0]):
        x_p, x_s = _layer(x_p, x_s, sp, ss, rel_table, norm_mix_g[l], w_in[l], q_norm_a[l],
                          k_norm_a[l], q_norm_b[l], k_norm_b[l], sink_b[l], w_proj_a[l],
                          w_proj_b[l], w_out[l], norm_ffn_g[l], w_router[l], w_gate_e[l],
                          w_up_e[l], w_down_e[l])
    return x_p.reshape(bp, sp, d), x_s.reshape(bs, ss, d)
```

```python
import functools
import math

import jax
import jax.numpy as jnp
from jax import lax
from jax.experimental import pallas as pl
from jax.experimental.pallas import tpu as pltpu

HEAD_DIM = 128
DIL_WINDOWS = (128, 512, 2048)
DIL_RATES = (1, 4, 16)
N_DIL_GROUPS = 3
A_HEADS = 4
B_Q_HEADS = 8
B_KV_HEADS = 2
B_GROUP = B_Q_HEADS // B_KV_HEADS
B_RADIUS = 128
REL_BUCKETS = 32
REL_MAX_DIST = 1024
EC_CAPACITY = 2
NORM_EPS = 1e-6
NEG_INF = -1e30

A_QKV = N_DIL_GROUPS * A_HEADS * HEAD_DIM
A_OUT = A_HEADS * HEAD_DIM
B_Q = B_Q_HEADS * HEAD_DIM
B_KV = B_KV_HEADS * HEAD_DIM

LANES = 128
GROUP_COLS = A_HEADS * HEAD_DIM
VMEM_LIMIT = 60 * 1024 * 1024
INPROJ_CHUNK = 256
QKV_COLS = 3 * GROUP_COLS
ATTN_TQ = 512
ATTN_SUB = 128
ATTN_SUB_B = 128
ATTN_RES_PER_STEP = 4
COMPACT_CHUNKS = 8
ROW_DMA_UNROLL = 8


def _cparams(sem, vmem=VMEM_LIMIT):
    return pltpu.CompilerParams(dimension_semantics=sem, vmem_limit_bytes=vmem)


def _inproj_kernel(xp_ref, xs_ref, g_ref, w_ref, z_ref, h_scr, *, tm, n_p):
    def normalise(x_ref):
        x = x_ref[...]
        ms = jnp.mean(x * x, axis=-1, keepdims=True)
        h_scr[...] = (x * lax.rsqrt(ms + NORM_EPS) * g_ref[...]).astype(jnp.bfloat16)

    first_col = pl.program_id(1) == 0
    is_prompt = pl.program_id(0) < n_p
    pl.when(jnp.logical_and(first_col, is_prompt))(lambda: normalise(xp_ref))
    pl.when(jnp.logical_and(first_col, jnp.logical_not(is_prompt)))(lambda: normalise(xs_ref))

    z_ref[...] = jnp.dot(h_scr[...], w_ref[...],
                         preferred_element_type=jnp.float32).astype(jnp.bfloat16)


def _inproj_strided_kernel(xp_ref, xs_ref, g_ref, w_ref, z_ref, zs, *, dil, tm, n_p):
    tn = w_ref.shape[1]
    chunk = INPROJ_CHUNK
    rows = chunk // dil

    def run(x_ref):
        for ci in range(tm // chunk):
            x = x_ref[ci * chunk:(ci + 1) * chunk, :]
            ms = jnp.mean(x * x, axis=-1, keepdims=True)
            h = (x * lax.rsqrt(ms + NORM_EPS) * g_ref[...]).astype(jnp.bfloat16)
            z = jnp.dot(h, w_ref[...], preferred_element_type=jnp.float32)
            slab = zs.at[ci % 2]
            for c in range(tn // LANES):
                slab[c] = z[:, c * LANES:(c + 1) * LANES]
            for r in range(dil):
                for c in range(tn // LANES):
                    z_ref[ci * rows:(ci + 1) * rows, r * tn + c * LANES:r * tn + (c + 1) * LANES] = (
                        slab[c, pl.ds(r, rows, stride=dil), :].astype(jnp.bfloat16))

    is_prompt = pl.program_id(0) < n_p
    pl.when(is_prompt)(lambda: run(xp_ref))
    pl.when(jnp.logical_not(is_prompt))(lambda: run(xs_ref))


def _inproj(x_p, x_s, g, w_bf16, tm, tn, dil=1):
    d = x_p.shape[1]
    n_p = x_p.shape[0] // tm
    n = x_p.shape[0] + x_s.shape[0]
    cols = w_bf16.shape[1]
    assert dil == 1 or tn == cols
    if dil == 1:
        kern = functools.partial(_inproj_kernel, tm=tm, n_p=n_p)
        scratch = [pltpu.VMEM((tm, d), jnp.bfloat16)]
    else:
        kern = functools.partial(_inproj_strided_kernel, dil=dil, tm=tm, n_p=n_p)
        scratch = [pltpu.VMEM((2, tn // LANES, INPROJ_CHUNK, LANES), jnp.float32)]
    return pl.pallas_call(
        kern,
        grid=(n // tm, cols // tn),
        in_specs=[
            pl.BlockSpec((tm, d), lambda i, j: (jnp.minimum(i, n_p - 1), 0)),
            pl.BlockSpec((tm, d), lambda i, j: (jnp.maximum(i - n_p, 0), 0)),
            pl.BlockSpec((1, d), lambda i, j: (0, 0)),
            pl.BlockSpec((d, tn), lambda i, j: (0, j)),
        ],
        out_specs=pl.BlockSpec((tm // dil, dil * tn), lambda i, j: (i, j)),
        out_shape=jax.ShapeDtypeStruct((n // dil, dil * cols), jnp.bfloat16),
        scratch_shapes=scratch,
        compiler_params=_cparams(("parallel", "arbitrary")),
        name=f"inproj_d{dil}",
    )(x_p, x_s, g.reshape(1, d), w_bf16)


def _t5_bucket(rel):
    half = REL_BUCKETS // 2
    max_exact = half // 2
    n = jnp.abs(rel)
    base = jnp.where(rel > 0, half, 0)
    nf = jnp.maximum(n, 1).astype(jnp.float32)
    large = max_exact + (jnp.log(nf / max_exact) / math.log(REL_MAX_DIST / max_exact)
                         * (half - max_exact)).astype(jnp.int32)
    large = jnp.minimum(large, half - 1)
    return base + jnp.where(n < max_exact, n, large)


def _bias_tile(table_cols, dil, radius, tq):
    tk = tq + 2 * radius
    n_heads = table_cols.shape[1]
    rel = jnp.arange(-radius, radius + 1)
    vals = table_cols[_t5_bucket(rel * dil)].astype(jnp.float32).T
    period = tq + tk
    w = jnp.full((n_heads, period), NEG_INF, jnp.float32).at[:, :2 * radius + 1].set(vals)
    flat = jnp.tile(w, (1, tq))[:, :tq * (period - 1)]
    return flat.reshape(n_heads, tq, period - 1)[:, :, :tk]


def _head_norm(x, w):
    xf = x.astype(jnp.float32)
    ms = jnp.mean(xf * xf, axis=-1, keepdims=True)
    return xf * lax.rsqrt(ms + NORM_EPS) * w


def _attn_kernel(*refs, tq, sub, radius, heads, res_per_step, with_sink, with_lse, np_rows,
                 len_p, len_s):
    cur_ref, prev_ref, next_ref, bias_ref, qw_ref, kw_ref = refs[:6]
    pos = 6
    sink_ref = None
    if with_sink:
        sink_ref = refs[pos]
        pos += 1
    o_ref = refs[pos]
    lse_ref = refs[pos + 1] if with_lse else None

    win = sub + 2 * radius
    n_heads = len(heads)
    q0 = pl.program_id(1) * tq
    lo_p = (q0 // len_p) * len_p
    lo_s = np_rows + ((q0 - np_rows) // len_s) * len_s
    in_p = q0 < np_rows
    lo = jnp.where(in_p, lo_p, lo_s)
    hi = lo + jnp.where(in_p, len_p, len_s)
    key_iota = lax.broadcasted_iota(jnp.int32, (1, win), 1)
    lane = lax.broadcasted_iota(jnp.int32, (sub, LANES), 1)
    scale = HEAD_DIM ** -0.5
    qw = qw_ref[...]
    kw = kw_ref[...]

    def window(col):
        cs = slice(col, col + HEAD_DIM)
        return jnp.concatenate([prev_ref[:, cs], cur_ref[:, cs], next_ref[:, cs]], axis=0)

    kv_groups = {}
    for h, (_, k_off, v_off) in enumerate(heads):
        kv_groups.setdefault((k_off, v_off), []).append(h)

    def stack(pieces):
        return pieces[0] if len(pieces) == 1 else jnp.concatenate(pieces, axis=0)

    for rr in range(res_per_step):
        base = rr * QKV_COLS
        lse_tiles = [jnp.zeros((sub, LANES), jnp.float32) for _ in range(tq // sub)]
        for (k_off, v_off), hs in kv_groups.items():
            kh = _head_norm(window(base + k_off), kw).astype(jnp.bfloat16)
            vh = window(base + v_off)
            qs = [(_head_norm(cur_ref[:, base + heads[h][0]:base + heads[h][0] + HEAD_DIM], qw)
                   * scale).astype(jnp.bfloat16) for h in hs]
            bias = stack([bias_ref[h] for h in hs])
            if with_sink:
                sink = stack([sink_ref[h] for h in hs])
                is_sink = sink > NEG_INF
            for a in range(tq // sub):
                kpos = q0 + a * sub - radius + key_iota
                valid = (kpos >= lo) & (kpos < hi)
                s = lax.dot_general(stack([q[a * sub:(a + 1) * sub] for q in qs]),
                                    kh[a * sub:a * sub + win], (((1,), (1,)), ((), ())),
                                    preferred_element_type=jnp.float32)
                s = jnp.where(valid, s + bias, NEG_INF)
                if with_sink:
                    s = jnp.maximum(s, sink)
                m = jnp.max(s, axis=-1, keepdims=True)
                p = jnp.exp(s - m)
                l = jnp.sum(p, axis=-1, keepdims=True)
                if with_sink:
                    p = jnp.where(is_sink, 0.0, p)
                o = jnp.dot(p.astype(jnp.bfloat16), vh[a * sub:a * sub + win],
                            preferred_element_type=jnp.float32) / l
                lse = m + jnp.log(l)
                for gi, h in enumerate(hs):
                    oc = (rr * n_heads + h) * HEAD_DIM
                    o_ref[a * sub:(a + 1) * sub, oc:oc + HEAD_DIM] = (
                        o[gi * sub:(gi + 1) * sub].astype(o_ref.dtype))
                    if with_lse:
                        lse_tiles[a] = jnp.where(lane == h, lse[gi * sub:(gi + 1) * sub],
                                                 lse_tiles[a])
        if with_lse:
            for a in range(tq // sub):
                lse_ref[a * sub:(a + 1) * sub, rr * LANES:(rr + 1) * LANES] = lse_tiles[a]


def _banded_attention(zv, *, dil, col0, radius, tq, heads, res_per_step, bias, q_w, k_w, sink,
                      with_lse, np_tok, seq_p, seq_s):
    rows = zv.shape[0]
    ntok = rows * dil
    n_heads = len(heads)
    assert col0 % QKV_COLS == 0 and dil % res_per_step == 0
    assert dil == 1 or zv.shape[1] == dil * QKV_COLS
    blk0 = col0 // QKV_COLS
    nq = rows // tq
    hb = tq // radius
    n_halo = rows // radius
    sub = bias.shape[1]
    wblk = res_per_step * QKV_COLS
    const2 = lambda r, i: (0, 0)
    const3 = lambda r, i: (0, 0, 0)
    in_specs = [
        pl.BlockSpec((tq, wblk), lambda r, i: (i, blk0 + r)),
        pl.BlockSpec((radius, wblk), lambda r, i: (jnp.maximum(i * hb - 1, 0), blk0 + r)),
        pl.BlockSpec((radius, wblk), lambda r, i: (jnp.minimum((i + 1) * hb, n_halo - 1), blk0 + r)),
        pl.BlockSpec((n_heads, sub, sub + 2 * radius), const3),
        pl.BlockSpec((1, HEAD_DIM), const2),
        pl.BlockSpec((1, HEAD_DIM), const2),
    ]
    args = [zv, zv, zv, bias, q_w.reshape(1, HEAD_DIM), k_w.reshape(1, HEAD_DIM)]
    if sink is not None:
        win = sub + 2 * radius
        row = jnp.arange(sub)
        col = jnp.where(row + 2 * radius + 1 < win, row + 2 * radius + 1, row - 1)
        at_col = jnp.arange(win)[None, :] == col[:, None]
        in_specs.append(pl.BlockSpec((n_heads, sub, win), const3))
        args.append(jnp.where(at_col[None], sink.astype(jnp.float32)[:, None, None], NEG_INF))
    omap = lambda r, i: (i, r)
    out_specs = [pl.BlockSpec((tq, res_per_step * n_heads * HEAD_DIM), omap)]
    out_shape = [jax.ShapeDtypeStruct((rows, dil * n_heads * HEAD_DIM), jnp.bfloat16)]
    if with_lse:
        out_specs.append(pl.BlockSpec((tq, res_per_step * LANES), omap))
        out_shape.append(jax.ShapeDtypeStruct((rows, dil * LANES), jnp.float32))
    kern = functools.partial(
        _attn_kernel, tq=tq, sub=sub, radius=radius, heads=heads, res_per_step=res_per_step,
        with_sink=sink is not None, with_lse=with_lse, np_rows=np_tok // dil,
        len_p=seq_p // dil, len_s=seq_s // dil)
    outs = pl.pallas_call(
        kern, grid=(dil // res_per_step, nq), in_specs=in_specs, out_specs=out_specs,
        out_shape=out_shape, compiler_params=_cparams(("parallel", "arbitrary")),
        name=f"band_attn_d{dil}_r{radius}",
    )(*args)
    return tuple(outs) if with_lse else outs[0]


def _merge_kernel(xp_ref, xs_ref, oa0_ref, oa1_ref, oa2_ref, l0_ref, l1_ref, l2_ref, ob_ref, ga0_ref,
                  ga1_ref, gb0_ref, gb1_ref, wpa_ref, wpb_ref, wo_ref, g2_ref, wr_ref, x1p_ref,
                  x1s_ref, h2p_ref, h2s_ref, aff_ref, oa_scr, l_scr, merged_scr, *, tm, n_p):
    step = pl.program_id(0)
    is_prompt = jnp.maximum(step - 1, 0) < n_p

    @pl.when(step == 0)
    def _():
        merged_scr[...] = jnp.zeros_like(merged_scr)

    delta = jnp.dot(merged_scr[...], wo_ref[...], preferred_element_type=jnp.float32)
    ga = jnp.concatenate([ga0_ref[...], ga1_ref[...]], axis=1)
    gb = jnp.concatenate([gb0_ref[...], gb1_ref[...]], axis=1)

    def natural_order(g, o_ref, l_ref):
        dil = DIL_RATES[g]
        if dil == 1:
            return ([o_ref[:, h * HEAD_DIM:(h + 1) * HEAD_DIM].astype(jnp.float32)
                     for h in range(A_HEADS)], l_ref[...])
        rows = tm // dil
        for r in range(dil):
            for h in range(A_HEADS):
                c0 = r * A_OUT + h * HEAD_DIM
                oa_scr[g, h, pl.ds(r, rows, stride=dil), :] = (
                    o_ref[:, c0:c0 + HEAD_DIM].astype(jnp.float32))
            l_scr[g, pl.ds(r, rows, stride=dil), :] = l_ref[:, r * LANES:(r + 1) * LANES]
        return [oa_scr[g, h] for h in range(A_HEADS)], l_scr[g]

    (o0, l0), (o1, l1), (o2, l2) = (natural_order(0, oa0_ref, l0_ref),
                                    natural_order(1, oa1_ref, l1_ref),
                                    natural_order(2, oa2_ref, l2_ref))
    mx = jnp.maximum(jnp.maximum(l0, l1), l2)
    e0, e1, e2 = jnp.exp(l0 - mx), jnp.exp(l1 - mx), jnp.exp(l2 - mx)
    den = e0 + e1 + e2
    w0, w1, w2 = e0 / den, e1 / den, e2 / den
    parts = [w0[:, h:h + 1] * o0[h] + w1[:, h:h + 1] * o1[h] + w2[:, h:h + 1] * o2[h]
             for h in range(A_HEADS)]
    o_a = jnp.concatenate(parts, axis=1).astype(jnp.bfloat16)
    pa = jnp.dot(o_a, wpa_ref[...], preferred_element_type=jnp.float32)
    pb = jnp.dot(ob_ref[...], wpb_ref[...], preferred_element_type=jnp.float32)
    merged = (jax.nn.sigmoid(ga.astype(jnp.float32)) * pa
              + jax.nn.sigmoid(gb.astype(jnp.float32)) * pb)
    merged_scr[...] = merged.astype(jnp.bfloat16)

    def finish(x_ref, x1_ref, h2_ref):
        x1 = x_ref[...] + delta
        x1_ref[...] = x1
        ms = jnp.mean(x1 * x1, axis=-1, keepdims=True)
        h2 = x1 * lax.rsqrt(ms + NORM_EPS) * g2_ref[...]
        h2_ref[...] = h2
        logits = lax.dot_general(wr_ref[...], h2.astype(jnp.bfloat16), (((1,), (1,)), ((), ())),
                                 preferred_element_type=jnp.float32)
        mx2 = jnp.max(logits, axis=0, keepdims=True)
        ex = jnp.exp(logits - mx2)
        aff = ex / jnp.sum(ex, axis=0, keepdims=True)
        for j in range(tm // LANES):
            aff_ref[j] = aff[:, j * LANES:(j + 1) * LANES]

    pl.when(is_prompt)(lambda: finish(xp_ref, x1p_ref, h2p_ref))
    pl.when(jnp.logical_not(is_prompt))(lambda: finish(xs_ref, x1s_ref, h2s_ref))


def _merge(x_p, x_s, oas, lses, ob, z, wpa, wpb, wo, g2, wr_t, tm):
    d = x_p.shape[1]
    n_p = x_p.shape[0] // tm
    n = x_p.shape[0] + x_s.shape[0]
    n_exp = wr_t.shape[0]
    half = d // 2
    assert (2 * QKV_COLS) % half == 0
    ga_blk = 2 * QKV_COLS // half
    n_tiles = n // tm
    formed = lambda i: jnp.minimum(i, n_tiles - 1)
    finished = lambda i: jnp.maximum(i - 1, 0)
    gate = lambda k: pl.BlockSpec((tm, half), lambda i: (formed(i), ga_blk + k))
    row = lambda i: (formed(i), 0)
    const = lambda i: (0, 0)
    row_p = lambda i: (jnp.minimum(finished(i), n_p - 1), 0)
    row_s = lambda i: (jnp.maximum(finished(i) - n_p, 0), 0)
    in_specs = [
        pl.BlockSpec((tm, d), row_p),
        pl.BlockSpec((tm, d), row_s),
        *[pl.BlockSpec((tm // dl, dl * A_OUT), row) for dl in DIL_RATES],
        *[pl.BlockSpec((tm // dl, dl * LANES), row) for dl in DIL_RATES],
        pl.BlockSpec((tm, B_Q), row),
        gate(0), gate(1), gate(2), gate(3),
        pl.BlockSpec((A_OUT, d), const),
        pl.BlockSpec((B_Q, d), const),
        pl.BlockSpec((d, d), const),
        pl.BlockSpec((1, d), const),
        pl.BlockSpec((n_exp, d), const),
    ]
    out_specs = [
        pl.BlockSpec((tm, d), row_p),
        pl.BlockSpec((tm, d), row_s),
        pl.BlockSpec((tm, d), row_p),
        pl.BlockSpec((tm, d), row_s),
        pl.BlockSpec((tm // LANES, n_exp, LANES), lambda i: (finished(i), 0, 0)),
    ]
    out_shape = [
        jax.ShapeDtypeStruct(x_p.shape, jnp.float32),
        jax.ShapeDtypeStruct(x_s.shape, jnp.float32),
        jax.ShapeDtypeStruct(x_p.shape, jnp.float32),
        jax.ShapeDtypeStruct(x_s.shape, jnp.float32),
        jax.ShapeDtypeStruct((n // LANES, n_exp, LANES), jnp.float32),
    ]
    return pl.pallas_call(
        functools.partial(_merge_kernel, tm=tm, n_p=n_p),
        grid=(n_tiles + 1,), in_specs=in_specs, out_specs=out_specs, out_shape=out_shape,
        scratch_shapes=[pltpu.VMEM((N_DIL_GROUPS, A_HEADS, tm, LANES), jnp.float32),
                        pltpu.VMEM((N_DIL_GROUPS, tm, LANES), jnp.float32),
                        pltpu.VMEM((tm, d), jnp.bfloat16)],
        compiler_params=_cparams(("arbitrary",)),
        name="merge_proj_router",
    )(x_p, x_s, oas[0], oas[1], oas[2], lses[0], lses[1], lses[2], ob, z, z, z, z, wpa, wpb, wo,
      g2.reshape(1, d), wr_t)


def _prefix_counts(flag_f32, tri, tot_scr, off_scr, nc):
    n_exp = flag_f32.shape[1]
    incl = jnp.dot(flag_f32.astype(jnp.bfloat16).reshape(nc * n_exp, LANES), tri,
                   preferred_element_type=jnp.float32).reshape(nc, n_exp, LANES)
    tot_scr[...] = jnp.broadcast_to(incl[:, :, LANES - 1:LANES], (nc, n_exp, LANES))

    def body(c, run):
        off_scr[c] = run
        return run + tot_scr[c]

    lax.fori_loop(0, nc, body, jnp.zeros((n_exp, LANES), jnp.float32))
    return off_scr[...] + incl - flag_f32


def _route_kernel(aff_ref, pos_ref, off_ref, tot_scr, off_scr, *, cap, nc):
    n_exp = aff_ref.shape[1]
    capf = jnp.float32(cap)

    def count(mask):
        c = jnp.sum(jnp.where(mask, 1.0, 0.0), axis=0)
        return jnp.sum(c, axis=-1, keepdims=True)

    def bit_body(k, t):
        cand = t | jnp.left_shift(jnp.int32(1), 30 - k)
        bits = pltpu.bitcast(aff_ref[...], jnp.int32)
        return jnp.where(count(bits >= cand[None]) >= capf, cand, t)

    t = lax.fori_loop(0, 31, bit_body, jnp.zeros((n_exp, 1), jnp.int32))
    bits = pltpu.bitcast(aff_ref[...], jnp.int32)
    gt = bits > t[None]
    eq = bits == t[None]
    need = capf - count(gt)
    rows = lax.broadcasted_iota(jnp.int32, (LANES, LANES), 0)
    cols = lax.broadcasted_iota(jnp.int32, (LANES, LANES), 1)
    tri = jnp.where(rows <= cols, 1.0, 0.0).astype(jnp.bfloat16)
    eq_f = jnp.where(eq, 1.0, 0.0)
    tie_rank = _prefix_counts(eq_f, tri, tot_scr, off_scr, nc)
    sel = gt | (eq & (tie_rank < need[None]))
    sel_f = jnp.where(sel, 1.0, 0.0)
    slot = _prefix_counts(sel_f, tri, tot_scr, off_scr, nc)
    pos_ref[...] = jnp.where(sel, slot, -1.0).astype(jnp.int32)
    off_ref[...] = off_scr[...].astype(jnp.int32)


def _route(aff, cap):
    nc, n_exp, _ = aff.shape
    full = pl.BlockSpec((nc, n_exp, LANES), lambda: (0, 0, 0))
    return pl.pallas_call(
        functools.partial(_route_kernel, cap=cap, nc=nc),
        in_specs=[full], out_specs=[full, full],
        out_shape=[jax.ShapeDtypeStruct((nc, n_exp, LANES), jnp.int32)] * 2,
        scratch_shapes=[pltpu.VMEM((nc, n_exp, LANES), jnp.float32)] * 2,
        compiler_params=pltpu.CompilerParams(vmem_limit_bytes=VMEM_LIMIT),
        name="route_select",
    )(aff)


def _split3(x):
    a = x.astype(jnp.bfloat16).astype(jnp.float32)
    r = x - a
    b = r.astype(jnp.bfloat16).astype(jnp.float32)
    c = r - b
    return a, b, c


def _compact_kernel(off_smem, pos_ref, aff_ref, idx_ref, gate_ref, acc_scr, *, nc, n_blk, tok_base):
    e = pl.program_id(0)
    stride = nc + 1
    sub = lax.broadcasted_iota(jnp.int32, (LANES, LANES), 0)
    sub16 = lax.broadcasted_iota(jnp.int32, (16, LANES), 0)
    lane = lax.broadcasted_iota(jnp.int32, (1, LANES), 1)

    n_grp = nc // COMPACT_CHUNKS

    def chunk_off(c):
        return off_smem[e * stride + jnp.minimum(c, nc)]

    def block_body(sb, g_first):
        lo_slot = sb * LANES

        def skip_cond(g):
            return jnp.logical_and(g < n_grp, chunk_off((g + 1) * COMPACT_CHUNKS) <= lo_slot)

        g_first = lax.while_loop(skip_cond, lambda g: g + 1, g_first)
        acc_scr[...] = jnp.zeros_like(acc_scr)

        def take_cond(g):
            return jnp.logical_and(g < n_grp, chunk_off(g * COMPACT_CHUNKS) < lo_slot + LANES)

        def take(g):
            total = jnp.zeros((16, LANES), jnp.float32)
            for k in range(COMPACT_CHUNKS):
                c = g * COMPACT_CHUNKS + k
                rel = pos_ref[c, pl.ds(e, 1), :] - lo_slot
                onehot = jnp.where(sub == rel, 1.0, 0.0).astype(jnp.bfloat16)
                tok = tok_base + c * LANES + lane
                g1, g2, g3 = _split3(aff_ref[c, pl.ds(e, 1), :])
                lhs = jnp.where(sub16 == 0, (tok >> 8).astype(jnp.float32),
                      jnp.where(sub16 == 1, (tok & 255).astype(jnp.float32),
                      jnp.where(sub16 == 2, g1,
                      jnp.where(sub16 == 3, g2,
                      jnp.where(sub16 == 4, g3, 0.0))))).astype(jnp.bfloat16)
                total = total + lax.dot_general(lhs, onehot, (((1,), (1,)), ((), ())),
                                                preferred_element_type=jnp.float32)
            acc_scr[...] += total
            return g + 1

        lax.while_loop(take_cond, take, g_first)
        acc = acc_scr[...]
        idx_ref[0, pl.ds(sb, 1), :] = (acc[0:1] * 256.0 + acc[1:2]).astype(jnp.int32)
        gate_ref[0, pl.ds(sb, 1), :] = (acc[2:3] + acc[3:4]) + acc[4:5]
        return g_first

    lax.fori_loop(0, n_blk, block_body, jnp.int32(0))


def _compact(offs_flat, pos, aff, cap, tok_base):
    nc, n_exp, _ = pos.shape
    n_blk = cap // LANES
    full = pl.BlockSpec((nc, n_exp, LANES), lambda e, off: (0, 0, 0))
    out = pl.BlockSpec((1, n_blk, LANES), lambda e, off: (e, 0, 0))
    return pl.pallas_call(
        functools.partial(_compact_kernel, nc=nc, n_blk=n_blk, tok_base=tok_base),
        grid_spec=pltpu.PrefetchScalarGridSpec(
            num_scalar_prefetch=1, grid=(n_exp,), in_specs=[full, full], out_specs=[out, out],
            scratch_shapes=[pltpu.VMEM((16, LANES), jnp.float32)]),
        out_shape=[jax.ShapeDtypeStruct((n_exp, n_blk, LANES), jnp.int32),
                   jax.ShapeDtypeStruct((n_exp, n_blk, LANES), jnp.float32)],
        compiler_params=_cparams(("arbitrary",)),
        name="route_compact",
    )(offs_flat, pos, aff)


def _route_set(aff, cap, tok_base):
    nc, n_exp, _ = aff.shape
    pos, off = _route(aff, cap)
    offs = jnp.concatenate([off[:, :, 0].T, jnp.full((n_exp, 1), cap, jnp.int32)], axis=1)
    idx, gate = _compact(offs.reshape(-1), pos, aff, cap, tok_base)
    return idx.reshape(n_exp, cap), gate.reshape(n_exp, cap)


MOE_ROWS_PER_SINGLE_STEP = 128


def _moe_schedule(tm, n_f):
    rest = n_f - 6
    assert n_f % 2 == 0 and rest > 0 and tm % 4 == 0 and (tm // 2) % rest == 0
    return ((4, (("scatter", tm // 4),)),
            (2, (("next", tm // 4),)),
            (rest, (("residual", tm // rest), ("next", tm // 2 // rest))))


def _moe_kernel(idx_prev, idx_cur, idx_nxt, gate_ref, wg_hbm, wu_hbm, wd_hbm, h2_hbm, x1_hbm,
                out_hbm, xe32, xe16, acc, orow, wg_buf, wu_buf, wd_buf, sems, wsems,
                *, tm, tf, nt, n_tiles, n_f):
    del x1_hbm
    n = pl.program_id(0)
    sem_x, sem_g, sem_s = sems.at[0], sems.at[1], sems.at[2]
    last_step = n_tiles * n_f - 1

    def weight_copies(step, slot):
        step = jnp.minimum(step, last_step)
        e = step // (nt * n_f)
        col = pl.multiple_of((step % n_f) * tf, tf)
        return (
            pltpu.make_async_copy(wg_hbm.at[e, :, pl.ds(col, tf)], wg_buf.at[slot], wsems.at[0, slot]),
            pltpu.make_async_copy(wu_hbm.at[e, :, pl.ds(col, tf)], wu_buf.at[slot], wsems.at[1, slot]),
            pltpu.make_async_copy(wd_hbm.at[e, pl.ds(col, tf), :], wd_buf.at[slot], wsems.at[2, slot]),
        )

    def xe_row(idx_smem, s):
        return pltpu.make_async_copy(h2_hbm.at[pl.ds(idx_smem[0, 0, s], 1)], xe32.at[pl.ds(s, 1)], sem_x)

    def residual_row(s):
        return pltpu.make_async_copy(out_hbm.at[pl.ds(idx_cur[0, 0, s], 1)], orow.at[pl.ds(s, 1)], sem_g)

    def scatter_row(idx_smem, s):
        return pltpu.make_async_copy(orow.at[pl.ds(s, 1)], out_hbm.at[pl.ds(idx_smem[0, 0, s], 1)], sem_s)

    def wait_xe():
        pltpu.make_async_copy(h2_hbm.at[pl.ds(0, tm)], xe32, sem_x).wait()

    def wait_scatter():
        pltpu.make_async_copy(orow, out_hbm.at[pl.ds(0, tm)], sem_s).wait()

    def ffn_step(f, slot=None):
        if slot is None:
            slot = f % 2
        for c in weight_copies(n * n_f + f, slot):
            c.wait()
        for c in weight_copies(n * n_f + f + 1, 1 - slot):
            c.start()
        x = xe16[...]
        gp = jnp.dot(x, wg_buf[slot].astype(jnp.bfloat16), preferred_element_type=jnp.float32)
        up = jnp.dot(x, wu_buf[slot].astype(jnp.bfloat16), preferred_element_type=jnp.float32)
        hid = (gp * jax.nn.sigmoid(gp) * up).astype(jnp.bfloat16)
        acc[...] += jnp.dot(hid, wd_buf[slot].astype(jnp.bfloat16),
                            preferred_element_type=jnp.float32)

    def run_steps(f_lo, count, row_work, per_iter=1):
        assert per_iter == 1 or (per_iter == 2 and f_lo % 2 == 0)

        def body(p, _):
            for u in range(per_iter):
                ffn_step(f_lo + per_iter * p + u, u if per_iter == 2 else None)
                row_work(per_iter * p + u)
            return 0
        lax.fori_loop(0, count // per_iter, body, 0)

    @pl.when(n == 0)
    def _():
        def body(s, _):
            xe_row(idx_cur, s).start()
            return 0
        lax.fori_loop(0, tm, body, 0, unroll=ROW_DMA_UNROLL)
        for c in weight_copies(0, 0):
            c.start()

    wait_xe()
    xe16[...] = xe32[...].astype(jnp.bfloat16)
    acc[...] = jnp.zeros_like(acc)

    f_lo = 0
    issued = {"scatter": 0, "next": 0, "residual": 0}
    for steps, works in _moe_schedule(tm, n_f):
        kinds = [kind for kind, _ in works]

        def row_work(k, works=works, base=dict(issued)):
            for kind, rows in works:
                for r in range(rows):
                    s = base[kind] + k * rows + r
                    if kind == "scatter":
                        scatter_row(idx_prev, s).start()
                    elif kind == "next":
                        xe_row(idx_nxt, s).start()
                    else:
                        residual_row(s).start()

        if "residual" in kinds:
            assert issued["scatter"] == tm and issued["residual"] == 0
            pl.when(n > 0)(wait_scatter)
        per_iter = 2 if sum(rows for _, rows in works) > MOE_ROWS_PER_SINGLE_STEP else 1
        if "scatter" in kinds:
            assert len(kinds) == 1
            pl.when(n > 0)(functools.partial(run_steps, f_lo, steps, row_work, per_iter))
            pl.when(n == 0)(functools.partial(run_steps, f_lo, steps, lambda k: None))
        else:
            run_steps(f_lo, steps, row_work, per_iter)
        for kind, rows in works:
            issued[kind] += steps * rows
        f_lo += steps
    assert f_lo == n_f and all(v == tm for v in issued.values())

    pltpu.make_async_copy(out_hbm.at[pl.ds(0, tm)], orow, sem_g).wait()
    g_t = gate_ref[0].T
    for j in range(tm // LANES):
        rs = slice(j * LANES, (j + 1) * LANES)
        orow[rs, :] = orow[rs, :] + acc[rs, :] * g_t[:, j:j + 1]

    @pl.when(n == n_tiles - 1)
    def _():
        def body(s, _):
            scatter_row(idx_cur, s).start()
            return 0
        lax.fori_loop(0, tm, body, 0, unroll=ROW_DMA_UNROLL)
        wait_scatter()
        wait_xe()
        for c in weight_copies(last_step, 0):
            c.wait()


def _moe(idx, gate, h2, x1, wg, wu, wd, tm, tf):
    n_exp, slots = idx.shape
    ntok, d = x1.shape
    ff = wg.shape[2]
    nt = slots // tm
    n_f = ff // tf
    n_tiles = n_exp * nt
    _moe_schedule(tm, n_f)
    idx3 = idx.reshape(n_tiles, 1, tm)
    gate3 = gate.reshape(n_exp, slots // LANES, LANES)
    smem = pltpu.MemorySpace.SMEM
    hbm = pl.BlockSpec(memory_space=pl.ANY)
    in_specs = [
        pl.BlockSpec((1, 1, tm), lambda n: (jnp.maximum(n - 1, 0), 0, 0), memory_space=smem),
        pl.BlockSpec((1, 1, tm), lambda n: (n, 0, 0), memory_space=smem),
        pl.BlockSpec((1, 1, tm), lambda n: (jnp.minimum(n + 1, n_tiles - 1), 0, 0), memory_space=smem),
        pl.BlockSpec((1, tm // LANES, LANES), lambda n: (n // nt, n % nt, 0)),
        hbm, hbm, hbm, hbm, hbm,
    ]
    return pl.pallas_call(
        functools.partial(_moe_kernel, tm=tm, tf=tf, nt=nt, n_tiles=n_tiles, n_f=n_f),
        grid=(n_tiles,), in_specs=in_specs, out_specs=hbm,
        out_shape=jax.ShapeDtypeStruct((ntok, d), jnp.float32),
        scratch_shapes=[
            pltpu.VMEM((tm, d), jnp.float32),
            pltpu.VMEM((tm, d), jnp.bfloat16),
            pltpu.VMEM((tm, d), jnp.float32),
            pltpu.VMEM((tm, d), jnp.float32),
            pltpu.VMEM((2, d, tf), jnp.float32),
            pltpu.VMEM((2, d, tf), jnp.float32),
            pltpu.VMEM((2, tf, d), jnp.float32),
            pltpu.SemaphoreType.DMA((3,)),
            pltpu.SemaphoreType.DMA((3, 2)),
        ],
        input_output_aliases={8: 0},
        compiler_params=_cparams(("arbitrary",)),
        name="expert_ffn",
    )(idx3, idx3, idx3, gate3, wg, wu, wd, h2, x1)


def _pick(n, candidates):
    for c in candidates:
        if n % c == 0:
            return c
    raise ValueError(f"no tile among {candidates} divides {n}")


def _layer(x_p, x_s, seq_p, seq_s, rel_table, norm_mix_g, w_in, q_norm_a, k_norm_a, q_norm_b,
           k_norm_b, sink_b, w_proj_a, w_proj_b, w_out, norm_ffn_g, w_router, w_gate_e, w_up_e,
           w_down_e):
    np_tok, d = x_p.shape
    ns_tok = x_s.shape[0]
    ntok = np_tok + ns_tok
    bf = jnp.bfloat16
    tm_in = _pick(math.gcd(np_tok, ns_tok), (1024, 512, 256))

    def group_cols(g):
        return [w_in[:, s * A_QKV + g * GROUP_COLS: s * A_QKV + (g + 1) * GROUP_COLS] for s in range(3)]

    w_nat = jnp.concatenate(group_cols(0) + [w_in[:, 3 * A_QKV:]], axis=1).astype(bf)
    z = _inproj(x_p, x_s, norm_mix_g, w_nat, tm_in, _pick(w_nat.shape[1], (1024, 512)))

    heads_a = tuple((h * HEAD_DIM, GROUP_COLS + h * HEAD_DIM, 2 * GROUP_COLS + h * HEAD_DIM)
                    for h in range(A_HEADS))
    heads_b = tuple((h * HEAD_DIM, B_Q + (h // B_GROUP) * HEAD_DIM,
                     B_Q + B_KV + (h // B_GROUP) * HEAD_DIM) for h in range(B_Q_HEADS))
    oas, lses = [], []
    for g in range(N_DIL_GROUPS):
        dil = DIL_RATES[g]
        radius = (DIL_WINDOWS[g] // 2) // dil
        tq = min(ATTN_TQ, seq_p // dil, seq_s // dil)
        bias = _bias_tile(rel_table[:, g * A_HEADS:(g + 1) * A_HEADS], dil, radius,
                          min(ATTN_SUB, tq))
        if dil == 1:
            zv = z
        else:
            w_g = jnp.concatenate(group_cols(g), axis=1).astype(bf)
            zv = _inproj(x_p, x_s, norm_mix_g, w_g, tm_in, w_g.shape[1], dil=dil)
        o, lse = _banded_attention(
            zv, dil=dil, col0=0, radius=radius, tq=tq, heads=heads_a,
            res_per_step=ATTN_RES_PER_STEP if (dil > 1 and tq <= ATTN_SUB) else 1, bias=bias,
            q_w=q_norm_a, k_w=k_norm_a, sink=None, with_lse=True, np_tok=np_tok, seq_p=seq_p,
            seq_s=seq_s)
        oas.append(o)
        lses.append(lse)
    tq_b = min(ATTN_TQ, seq_p, seq_s)
    bias_b = _bias_tile(rel_table[:, N_DIL_GROUPS * A_HEADS:], 1, B_RADIUS, min(ATTN_SUB_B, tq_b))
    ob = _banded_attention(
        z, dil=1, col0=QKV_COLS, radius=B_RADIUS, tq=tq_b, heads=heads_b, res_per_step=1,
        bias=bias_b, q_w=q_norm_b, k_w=k_norm_b, sink=sink_b, with_lse=False, np_tok=np_tok,
        seq_p=seq_p, seq_s=seq_s)

    x1_p, x1_s, h2_p, h2_s, aff = _merge(
        x_p, x_s, oas, lses, ob, z, w_proj_a.astype(bf), w_proj_b.astype(bf), w_out.astype(bf),
        norm_ffn_g, w_router.T.astype(bf), _pick(math.gcd(np_tok, ns_tok), (256,)))

    n_exp = w_router.shape[1]
    tf = _pick(w_gate_e.shape[2], (256, 128))
    outs = []
    for x1, h2, aff_set in ((x1_p, h2_p, aff[:np_tok // LANES]), (x1_s, h2_s, aff[np_tok // LANES:])):
        cap = max(1, EC_CAPACITY * x1.shape[0] // n_exp)
        idx, gate = _route_set(aff_set, cap, 0)
        outs.append(_moe(idx, gate, h2, x1, w_gate_e, w_up_e, w_down_e, _pick(cap, (1024,)), tf))
    return outs


def kernel(x_prompt, x_sample, rel_table, norm_mix_g, w_in, q_norm_a, k_norm_a, q_norm_b, k_norm_b,
           sink_b, w_proj_a, w_proj_b, w_out, norm_ffn_g, w_router, w_gate_e, w_up_e, w_down_e):
    bp, sp, d = x_prompt.shape
    bs, ss, _ = x_sample.shape
    x_p = x_prompt.reshape(bp * sp, d)
    x_s = x_sample.reshape(bs * ss, d)
    for l in range(norm_mix_g.shape[0]):
        x_p, x_s = _layer(x_p, x_s, sp, ss, rel_table, norm_mix_g[l], w_in[l], q_norm_a[l],
                          k_norm_a[l], q_norm_b[l], k_norm_b[l], sink_b[l], w_proj_a[l],
                          w_proj_b[l], w_out[l], norm_ffn_g[l], w_router[l], w_gate_e[l],
                          w_up_e[l], w_down_e[l])
    return x_p.reshape(bp, sp, d), x_s.reshape(bs, ss, d)
```

```python
import functools
import math

import jax
import jax.numpy as jnp
from jax import lax
from jax.experimental import pallas as pl
from jax.experimental.pallas import tpu as pltpu

HEAD_DIM = 128
DIL_WINDOWS = (128, 512, 2048)
DIL_RATES = (1, 4, 16)
N_DIL_GROUPS = 3
A_HEADS = 4
B_Q_HEADS = 8
B_KV_HEADS = 2
B_GROUP = B_Q_HEADS // B_KV_HEADS
B_RADIUS = 128
REL_BUCKETS = 32
REL_MAX_DIST = 1024
EC_CAPACITY = 2
NORM_EPS = 1e-6
NEG_INF = -1e30

A_QKV = N_DIL_GROUPS * A_HEADS * HEAD_DIM
A_OUT = A_HEADS * HEAD_DIM
B_Q = B_Q_HEADS * HEAD_DIM
B_KV = B_KV_HEADS * HEAD_DIM

LANES = 128
GROUP_COLS = A_HEADS * HEAD_DIM
VMEM_LIMIT = 60 * 1024 * 1024
INPROJ_CHUNK = 256
QKV_COLS = 3 * GROUP_COLS
ATTN_TQ = 512
ATTN_SUB = 128
ATTN_SUB_B = 128
ATTN_RES_PER_STEP = 4
COMPACT_CHUNKS = 8
ROW_DMA_UNROLL = 8


def _cparams(sem, vmem=VMEM_LIMIT):
    return pltpu.CompilerParams(dimension_semantics=sem, vmem_limit_bytes=vmem)


def _inproj_kernel(xp_ref, xs_ref, g_ref, w_ref, z_ref, h_scr, *, tm, n_p):
    def normalise(x_ref):
        x = x_ref[...]
        ms = jnp.mean(x * x, axis=-1, keepdims=True)
        h_scr[...] = (x * lax.rsqrt(ms + NORM_EPS) * g_ref[...]).astype(jnp.bfloat16)

    first_col = pl.program_id(1) == 0
    is_prompt = pl.program_id(0) < n_p
    pl.when(jnp.logical_and(first_col, is_prompt))(lambda: normalise(xp_ref))
    pl.when(jnp.logical_and(first_col, jnp.logical_not(is_prompt)))(lambda: normalise(xs_ref))

    z_ref[...] = jnp.dot(h_scr[...], w_ref[...],
                         preferred_element_type=jnp.float32).astype(jnp.bfloat16)


def _inproj_strided_kernel(xp_ref, xs_ref, g_ref, w_ref, z_ref, zs, *, dil, tm, n_p):
    tn = w_ref.shape[1]
    chunk = INPROJ_CHUNK
    rows = chunk // dil

    def run(x_ref):
        for ci in range(tm // chunk):
            x = x_ref[ci * chunk:(ci + 1) * chunk, :]
            ms = jnp.mean(x * x, axis=-1, keepdims=True)
            h = (x * lax.rsqrt(ms + NORM_EPS) * g_ref[...]).astype(jnp.bfloat16)
            z = jnp.dot(h, w_ref[...], preferred_element_type=jnp.float32)
            slab = zs.at[ci % 2]
            for c in range(tn // LANES):
                slab[c] = z[:, c * LANES:(c + 1) * LANES]
            for r in range(dil):
                for c in range(tn // LANES):
                    z_ref[ci * rows:(ci + 1) * rows, r * tn + c * LANES:r * tn + (c + 1) * LANES] = (
                        slab[c, pl.ds(r, rows, stride=dil), :].astype(jnp.bfloat16))

    is_prompt = pl.program_id(0) < n_p
    pl.when(is_prompt)(lambda: run(xp_ref))
    pl.when(jnp.logical_not(is_prompt))(lambda: run(xs_ref))


def _inproj(x_p, x_s, g, w_bf16, tm, tn, dil=1):
    d = x_p.shape[1]
    n_p = x_p.shape[0] // tm
    n = x_p.shape[0] + x_s.shape[0]
    cols = w_bf16.shape[1]
    assert dil == 1 or tn == cols
    if dil == 1:
        kern = functools.partial(_inproj_kernel, tm=tm, n_p=n_p)
        scratch = [pltpu.VMEM((tm, d), jnp.bfloat16)]
    else:
        kern = functools.partial(_inproj_strided_kernel, dil=dil, tm=tm, n_p=n_p)
        scratch = [pltpu.VMEM((2, tn // LANES, INPROJ_CHUNK, LANES), jnp.float32)]
    return pl.pallas_call(
        kern,
        grid=(n // tm, cols // tn),
        in_specs=[
            pl.BlockSpec((tm, d), lambda i, j: (jnp.minimum(i, n_p - 1), 0)),
            pl.BlockSpec((tm, d), lambda i, j: (jnp.maximum(i - n_p, 0), 0)),
            pl.BlockSpec((1, d), lambda i, j: (0, 0)),
            pl.BlockSpec((d, tn), lambda i, j: (0, j)),
        ],
        out_specs=pl.BlockSpec((tm // dil, dil * tn), lambda i, j: (i, j)),
        out_shape=jax.ShapeDtypeStruct((n // dil, dil * cols), jnp.bfloat16),
        scratch_shapes=scratch,
        compiler_params=_cparams(("parallel", "arbitrary")),
        name=f"inproj_d{dil}",
    )(x_p, x_s, g.reshape(1, d), w_bf16)


def _t5_bucket(rel):
    half = REL_BUCKETS // 2
    max_exact = half // 2
    n = jnp.abs(rel)
    base = jnp.where(rel > 0, half, 0)
    nf = jnp.maximum(n, 1).astype(jnp.float32)
    large = max_exact + (jnp.log(nf / max_exact) / math.log(REL_MAX_DIST / max_exact)
                         * (half - max_exact)).astype(jnp.int32)
    large = jnp.minimum(large, half - 1)
    return base + jnp.where(n < max_exact, n, large)


def _bias_tile(table_cols, dil, radius, tq):
    tk = tq + 2 * radius
    n_heads = table_cols.shape[1]
    rel = jnp.arange(-radius, radius + 1)
    vals = table_cols[_t5_bucket(rel * dil)].astype(jnp.float32).T
    period = tq + tk
    w = jnp.full((n_heads, period), NEG_INF, jnp.float32).at[:, :2 * radius + 1].set(vals)
    flat = jnp.tile(w, (1, tq))[:, :tq * (period - 1)]
    return flat.reshape(n_heads, tq, period - 1)[:, :, :tk]


def _head_norm(x, w):
    xf = x.astype(jnp.float32)
    ms = jnp.mean(xf * xf, axis=-1, keepdims=True)
    return xf * lax.rsqrt(ms + NORM_EPS) * w


def _attn_kernel(*refs, tq, sub, radius, heads, res_per_step, with_sink, with_lse, np_rows,
                 len_p, len_s):
    cur_ref, prev_ref, next_ref, bias_ref, qw_ref, kw_ref = refs[:6]
    pos = 6
    sink_ref = None
    if with_sink:
        sink_ref = refs[pos]
        pos += 1
    o_ref = refs[pos]
    lse_ref = refs[pos + 1] if with_lse else None

    win = sub + 2 * radius
    n_heads = len(heads)
    q0 = pl.program_id(1) * tq
    lo_p = (q0 // len_p) * len_p
    lo_s = np_rows + ((q0 - np_rows) // len_s) * len_s
    in_p = q0 < np_rows
    lo = jnp.where(in_p, lo_p, lo_s)
    hi = lo + jnp.where(in_p, len_p, len_s)
    key_iota = lax.broadcasted_iota(jnp.int32, (1, win), 1)
    lane = lax.broadcasted_iota(jnp.int32, (sub, LANES), 1)
    scale = HEAD_DIM ** -0.5
    qw = qw_ref[...]
    kw = kw_ref[...]

    def window(col):
        cs = slice(col, col + HEAD_DIM)
        return jnp.concatenate([prev_ref[:, cs], cur_ref[:, cs], next_ref[:, cs]], axis=0)

    kv_groups = {}
    for h, (_, k_off, v_off) in enumerate(heads):
        kv_groups.setdefault((k_off, v_off), []).append(h)

    def stack(pieces):
        return pieces[0] if len(pieces) == 1 else jnp.concatenate(pieces, axis=0)

    for rr in range(res_per_step):
        base = rr * QKV_COLS
        lse_tiles = [jnp.zeros((sub, LANES), jnp.float32) for _ in range(tq // sub)]
        for (k_off, v_off), hs in kv_groups.items():
            kh = _head_norm(window(base + k_off), kw).astype(jnp.bfloat16)
            vh = window(base + v_off)
            qs = [(_head_norm(cur_ref[:, base + heads[h][0]:base + heads[h][0] + HEAD_DIM], qw)
                   * scale).astype(jnp.bfloat16) for h in hs]
            bias = stack([bias_ref[h] for h in hs])
            if with_sink:
                sink = stack([sink_ref[h] for h in hs])
                is_sink = sink > NEG_INF
            for a in range(tq // sub):
                kpos = q0 + a * sub - radius + key_iota
                valid = (kpos >= lo) & (kpos < hi)
                s = lax.dot_general(stack([q[a * sub:(a + 1) * sub] for q in qs]),
                                    kh[a * sub:a * sub + win], (((1,), (1,)), ((), ())),
                                    preferred_element_type=jnp.float32)
                s = jnp.where(valid, s + bias, NEG_INF)
                if with_sink:
                    s = jnp.maximum(s, sink)
                m = jnp.max(s, axis=-1, keepdims=True)
                p = jnp.exp(s - m)
                l = jnp.sum(p, axis=-1, keepdims=True)
                if with_sink:
                    p = jnp.where(is_sink, 0.0, p)
                o = jnp.dot(p.astype(jnp.bfloat16), vh[a * sub:a * sub + win],
                            preferred_element_type=jnp.float32) / l
                lse = m + jnp.log(l)
                for gi, h in enumerate(hs):
                    oc = (rr * n_heads + h) * HEAD_DIM
                    o_ref[a * sub:(a + 1) * sub, oc:oc + HEAD_DIM] = (
                        o[gi * sub:(gi + 1) * sub].astype(o_ref.dtype))
                    if with_lse:
                        lse_tiles[a] = jnp.where(lane == h, lse[gi * sub:(gi + 1) * sub],
                                                 lse_tiles[a])
        if with_lse:
            for a in range(tq // sub):
                lse_ref[a * sub:(a + 1) * sub, rr * LANES:(rr + 1) * LANES] = lse_tiles[a]


def _banded_attention(zv, *, dil, col0, radius, tq, heads, res_per_step, bias, q_w, k_w, sink,
                      with_lse, np_tok, seq_p, seq_s):
    rows = zv.shape[0]
    ntok = rows * dil
    n_heads = len(heads)
    assert col0 % QKV_COLS == 0 and dil % res_per_step == 0
    assert dil == 1 or zv.shape[1] == dil * QKV_COLS
    blk0 = col0 // QKV_COLS
    nq = rows // tq
    hb = tq // radius
    n_halo = rows // radius
    sub = bias.shape[1]
    wblk = res_per_step * QKV_COLS
    const2 = lambda r, i: (0, 0)
    const3 = lambda r, i: (0, 0, 0)
    in_specs = [
        pl.BlockSpec((tq, wblk), lambda r, i: (i, blk0 + r)),
        pl.BlockSpec((radius, wblk), lambda r, i: (jnp.maximum(i * hb - 1, 0), blk0 + r)),
        pl.BlockSpec((radius, wblk), lambda r, i: (jnp.minimum((i + 1) * hb, n_halo - 1), blk0 + r)),
        pl.BlockSpec((n_heads, sub, sub + 2 * radius), const3),
        pl.BlockSpec((1, HEAD_DIM), const2),
        pl.BlockSpec((1, HEAD_DIM), const2),
    ]
    args = [zv, zv, zv, bias, q_w.reshape(1, HEAD_DIM), k_w.reshape(1, HEAD_DIM)]
    if sink is not None:
        win = sub + 2 * radius
        row = jnp.arange(sub)
        col = jnp.where(row + 2 * radius + 1 < win, row + 2 * radius + 1, row - 1)
        at_col = jnp.arange(win)[None, :] == col[:, None]
        in_specs.append(pl.BlockSpec((n_heads, sub, win), const3))
        args.append(jnp.where(at_col[None], sink.astype(jnp.float32)[:, None, None], NEG_INF))
    omap = lambda r, i: (i, r)
    out_specs = [pl.BlockSpec((tq, res_per_step * n_heads * HEAD_DIM), omap)]
    out_shape = [jax.ShapeDtypeStruct((rows, dil * n_heads * HEAD_DIM), jnp.bfloat16)]
    if with_lse:
        out_specs.append(pl.BlockSpec((tq, res_per_step * LANES), omap))
        out_shape.append(jax.ShapeDtypeStruct((rows, dil * LANES), jnp.float32))
    kern = functools.partial(
        _attn_kernel, tq=tq, sub=sub, radius=radius, heads=heads, res_per_step=res_per_step,
        with_sink=sink is not None, with_lse=with_lse, np_rows=np_tok // dil,
        len_p=seq_p // dil, len_s=seq_s // dil)
    outs = pl.pallas_call(
        kern, grid=(dil // res_per_step, nq), in_specs=in_specs, out_specs=out_specs,
        out_shape=out_shape, compiler_params=_cparams(("parallel", "arbitrary")),
        name=f"band_attn_d{dil}_r{radius}",
    )(*args)
    return tuple(outs) if with_lse else outs[0]


def _merge_kernel(xp_ref, xs_ref, oa0_ref, oa1_ref, oa2_ref, l0_ref, l1_ref, l2_ref, ob_ref, ga0_ref,
                  ga1_ref, gb0_ref, gb1_ref, wpa_ref, wpb_ref, wo_ref, g2_ref, wr_ref, x1p_ref,
                  x1s_ref, h2p_ref, h2s_ref, aff_ref, oa_scr, l_scr, merged_scr, *, tm, n_p):
    step = pl.program_id(0)
    is_prompt = jnp.maximum(step - 1, 0) < n_p

    @pl.when(step == 0)
    def _():
        merged_scr[...] = jnp.zeros_like(merged_scr)

    delta = jnp.dot(merged_scr[...], wo_ref[...], preferred_element_type=jnp.float32)
    ga = jnp.concatenate([ga0_ref[...], ga1_ref[...]], axis=1)
    gb = jnp.concatenate([gb0_ref[...], gb1_ref[...]], axis=1)

    def natural_order(g, o_ref, l_ref):
        dil = DIL_RATES[g]
        if dil == 1:
            return ([o_ref[:, h * HEAD_DIM:(h + 1) * HEAD_DIM].astype(jnp.float32)
                     for h in range(A_HEADS)], l_ref[...])
        rows = tm // dil
        for r in range(dil):
            for h in range(A_HEADS):
                c0 = r * A_OUT + h * HEAD_DIM
                oa_scr[g, h, pl.ds(r, rows, stride=dil), :] = (
                    o_ref[:, c0:c0 + HEAD_DIM].astype(jnp.float32))
            l_scr[g, pl.ds(r, rows, stride=dil), :] = l_ref[:, r * LANES:(r + 1) * LANES]
        return [oa_scr[g, h] for h in range(A_HEADS)], l_scr[g]

    (o0, l0), (o1, l1), (o2, l2) = (natural_order(0, oa0_ref, l0_ref),
                                    natural_order(1, oa1_ref, l1_ref),
                                    natural_order(2, oa2_ref, l2_ref))
    mx = jnp.maximum(jnp.maximum(l0, l1), l2)
    e0, e1, e2 = jnp.exp(l0 - mx), jnp.exp(l1 - mx), jnp.exp(l2 - mx)
    den = e0 + e1 + e2
    w0, w1, w2 = e0 / den, e1 / den, e2 / den
    parts = [w0[:, h:h + 1] * o0[h] + w1[:, h:h + 1] * o1[h] + w2[:, h:h + 1] * o2[h]
             for h in range(A_HEADS)]
    o_a = jnp.concatenate(parts, axis=1).astype(jnp.bfloat16)
    pa = jnp.dot(o_a, wpa_ref[...], preferred_element_type=jnp.float32)
    pb = jnp.dot(ob_ref[...], wpb_ref[...], preferred_element_type=jnp.float32)
    merged = (jax.nn.sigmoid(ga.astype(jnp.float32)) * pa
              + jax.nn.sigmoid(gb.astype(jnp.float32)) * pb)
    merged_scr[...] = merged.astype(jnp.bfloat16)

    def finish(x_ref, x1_ref, h2_ref):
        x1 = x_ref[...] + delta
        x1_ref[...] = x1
        ms = jnp.mean(x1 * x1, axis=-1, keepdims=True)
        h2 = (x1 * lax.rsqrt(ms + NORM_EPS) * g2_ref[...]).astype(jnp.bfloat16)
        bits = lax.bitcast_convert_type(h2.astype(jnp.float32), jnp.uint32)
        half = bits.shape[1] // 2
        h2_ref[...] = (bits[:, half:] & jnp.uint32(0xFFFF0000)) | (bits[:, :half] >> 16)
        logits = lax.dot_general(wr_ref[...], h2, (((1,), (1,)), ((), ())),
                                 preferred_element_type=jnp.float32)
        mx2 = jnp.max(logits, axis=0, keepdims=True)
        ex = jnp.exp(logits - mx2)
        aff = ex / jnp.sum(ex, axis=0, keepdims=True)
        for j in range(tm // LANES):
            aff_ref[j] = aff[:, j * LANES:(j + 1) * LANES]

    pl.when(is_prompt)(lambda: finish(xp_ref, x1p_ref, h2p_ref))
    pl.when(jnp.logical_not(is_prompt))(lambda: finish(xs_ref, x1s_ref, h2s_ref))


def _merge(x_p, x_s, oas, lses, ob, z, wpa, wpb, wo, g2, wr_t, tm):
    d = x_p.shape[1]
    n_p = x_p.shape[0] // tm
    n = x_p.shape[0] + x_s.shape[0]
    n_exp = wr_t.shape[0]
    half = d // 2
    assert (2 * QKV_COLS) % half == 0
    ga_blk = 2 * QKV_COLS // half
    n_tiles = n // tm
    formed = lambda i: jnp.minimum(i, n_tiles - 1)
    finished = lambda i: jnp.maximum(i - 1, 0)
    gate = lambda k: pl.BlockSpec((tm, half), lambda i: (formed(i), ga_blk + k))
    row = lambda i: (formed(i), 0)
    const = lambda i: (0, 0)
    row_p = lambda i: (jnp.minimum(finished(i), n_p - 1), 0)
    row_s = lambda i: (jnp.maximum(finished(i) - n_p, 0), 0)
    in_specs = [
        pl.BlockSpec((tm, d), row_p),
        pl.BlockSpec((tm, d), row_s),
        *[pl.BlockSpec((tm // dl, dl * A_OUT), row) for dl in DIL_RATES],
        *[pl.BlockSpec((tm // dl, dl * LANES), row) for dl in DIL_RATES],
        pl.BlockSpec((tm, B_Q), row),
        gate(0), gate(1), gate(2), gate(3),
        pl.BlockSpec((A_OUT, d), const),
        pl.BlockSpec((B_Q, d), const),
        pl.BlockSpec((d, d), const),
        pl.BlockSpec((1, d), const),
        pl.BlockSpec((n_exp, d), const),
    ]
    out_specs = [
        pl.BlockSpec((tm, d), row_p),
        pl.BlockSpec((tm, d), row_s),
        pl.BlockSpec((tm, half), row_p),
        pl.BlockSpec((tm, half), row_s),
        pl.BlockSpec((tm // LANES, n_exp, LANES), lambda i: (finished(i), 0, 0)),
    ]
    out_shape = [
        jax.ShapeDtypeStruct(x_p.shape, jnp.float32),
        jax.ShapeDtypeStruct(x_s.shape, jnp.float32),
        jax.ShapeDtypeStruct((x_p.shape[0], half), jnp.uint32),
        jax.ShapeDtypeStruct((x_s.shape[0], half), jnp.uint32),
        jax.ShapeDtypeStruct((n // LANES, n_exp, LANES), jnp.float32),
    ]
    return pl.pallas_call(
        functools.partial(_merge_kernel, tm=tm, n_p=n_p),
        grid=(n_tiles + 1,), in_specs=in_specs, out_specs=out_specs, out_shape=out_shape,
        scratch_shapes=[pltpu.VMEM((N_DIL_GROUPS, A_HEADS, tm, LANES), jnp.float32),
                        pltpu.VMEM((N_DIL_GROUPS, tm, LANES), jnp.float32),
                        pltpu.VMEM((tm, d), jnp.bfloat16)],
        compiler_params=_cparams(("arbitrary",)),
        name="merge_proj_router",
    )(x_p, x_s, oas[0], oas[1], oas[2], lses[0], lses[1], lses[2], ob, z, z, z, z, wpa, wpb, wo,
      g2.reshape(1, d), wr_t)


def _prefix_counts(flag_f32, tri, tot_scr, off_scr, nc):
    n_exp = flag_f32.shape[1]
    incl = jnp.dot(flag_f32.astype(jnp.bfloat16).reshape(nc * n_exp, LANES), tri,
                   preferred_element_type=jnp.float32).reshape(nc, n_exp, LANES)
    tot_scr[...] = jnp.broadcast_to(incl[:, :, LANES - 1:LANES], (nc, n_exp, LANES))

    def body(c, run):
        off_scr[c] = run
        return run + tot_scr[c]

    lax.fori_loop(0, nc, body, jnp.zeros((n_exp, LANES), jnp.float32))
    return off_scr[...] + incl - flag_f32


def _route_kernel(aff_ref, pos_ref, off_ref, tot_scr, off_scr, *, cap, nc):
    n_exp = aff_ref.shape[1]
    capf = jnp.float32(cap)

    def count(mask):
        c = jnp.sum(jnp.where(mask, 1.0, 0.0), axis=0)
        return jnp.sum(c, axis=-1, keepdims=True)

    def bit_body(k, t):
        cand = t | jnp.left_shift(jnp.int32(1), 30 - k)
        bits = pltpu.bitcast(aff_ref[...], jnp.int32)
        return jnp.where(count(bits >= cand[None]) >= capf, cand, t)

    t = lax.fori_loop(0, 31, bit_body, jnp.zeros((n_exp, 1), jnp.int32))
    bits = pltpu.bitcast(aff_ref[...], jnp.int32)
    gt = bits > t[None]
    eq = bits == t[None]
    need = capf - count(gt)
    rows = lax.broadcasted_iota(jnp.int32, (LANES, LANES), 0)
    cols = lax.broadcasted_iota(jnp.int32, (LANES, LANES), 1)
    tri = jnp.where(rows <= cols, 1.0, 0.0).astype(jnp.bfloat16)
    eq_f = jnp.where(eq, 1.0, 0.0)
    tie_rank = _prefix_counts(eq_f, tri, tot_scr, off_scr, nc)
    sel = gt | (eq & (tie_rank < need[None]))
    sel_f = jnp.where(sel, 1.0, 0.0)
    slot = _prefix_counts(sel_f, tri, tot_scr, off_scr, nc)
    pos_ref[...] = jnp.where(sel, slot, -1.0).astype(jnp.int32)
    off_ref[...] = off_scr[...].astype(jnp.int32)


def _route(aff, cap):
    nc, n_exp, _ = aff.shape
    full = pl.BlockSpec((nc, n_exp, LANES), lambda: (0, 0, 0))
    return pl.pallas_call(
        functools.partial(_route_kernel, cap=cap, nc=nc),
        in_specs=[full], out_specs=[full, full],
        out_shape=[jax.ShapeDtypeStruct((nc, n_exp, LANES), jnp.int32)] * 2,
        scratch_shapes=[pltpu.VMEM((nc, n_exp, LANES), jnp.float32)] * 2,
        compiler_params=pltpu.CompilerParams(vmem_limit_bytes=VMEM_LIMIT),
        name="route_select",
    )(aff)


def _split3(x):
    a = x.astype(jnp.bfloat16).astype(jnp.float32)
    r = x - a
    b = r.astype(jnp.bfloat16).astype(jnp.float32)
    c = r - b
    return a, b, c


def _compact_kernel(off_smem, pos_ref, aff_ref, idx_ref, gate_ref, acc_scr, *, nc, n_blk, tok_base):
    e = pl.program_id(0)
    stride = nc + 1
    sub = lax.broadcasted_iota(jnp.int32, (LANES, LANES), 0)
    sub16 = lax.broadcasted_iota(jnp.int32, (16, LANES), 0)
    lane = lax.broadcasted_iota(jnp.int32, (1, LANES), 1)

    n_grp = nc // COMPACT_CHUNKS

    def chunk_off(c):
        return off_smem[e * stride + jnp.minimum(c, nc)]

    def block_body(sb, g_first):
        lo_slot = sb * LANES

        def skip_cond(g):
            return jnp.logical_and(g < n_grp, chunk_off((g + 1) * COMPACT_CHUNKS) <= lo_slot)

        g_first = lax.while_loop(skip_cond, lambda g: g + 1, g_first)
        acc_scr[...] = jnp.zeros_like(acc_scr)

        def take_cond(g):
            return jnp.logical_and(g < n_grp, chunk_off(g * COMPACT_CHUNKS) < lo_slot + LANES)

        def take(g):
            total = jnp.zeros((16, LANES), jnp.float32)
            for k in range(COMPACT_CHUNKS):
                c = g * COMPACT_CHUNKS + k
                rel = pos_ref[c, pl.ds(e, 1), :] - lo_slot
                onehot = jnp.where(sub == rel, 1.0, 0.0).astype(jnp.bfloat16)
                tok = tok_base + c * LANES + lane
                g1, g2, g3 = _split3(aff_ref[c, pl.ds(e, 1), :])
                lhs = jnp.where(sub16 == 0, (tok >> 8).astype(jnp.float32),
                      jnp.where(sub16 == 1, (tok & 255).astype(jnp.float32),
                      jnp.where(sub16 == 2, g1,
                      jnp.where(sub16 == 3, g2,
                      jnp.where(sub16 == 4, g3, 0.0))))).astype(jnp.bfloat16)
                total = total + lax.dot_general(lhs, onehot, (((1,), (1,)), ((), ())),
                                                preferred_element_type=jnp.float32)
            acc_scr[...] += total
            return g + 1

        lax.while_loop(take_cond, take, g_first)
        acc = acc_scr[...]
        idx_ref[0, pl.ds(sb, 1), :] = (acc[0:1] * 256.0 + acc[1:2]).astype(jnp.int32)
        gate_ref[0, pl.ds(sb, 1), :] = (acc[2:3] + acc[3:4]) + acc[4:5]
        return g_first

    lax.fori_loop(0, n_blk, block_body, jnp.int32(0))


def _compact(offs_flat, pos, aff, cap, tok_base):
    nc, n_exp, _ = pos.shape
    n_blk = cap // LANES
    full = pl.BlockSpec((nc, n_exp, LANES), lambda e, off: (0, 0, 0))
    out = pl.BlockSpec((1, n_blk, LANES), lambda e, off: (e, 0, 0))
    return pl.pallas_call(
        functools.partial(_compact_kernel, nc=nc, n_blk=n_blk, tok_base=tok_base),
        grid_spec=pltpu.PrefetchScalarGridSpec(
            num_scalar_prefetch=1, grid=(n_exp,), in_specs=[full, full], out_specs=[out, out],
            scratch_shapes=[pltpu.VMEM((16, LANES), jnp.float32)]),
        out_shape=[jax.ShapeDtypeStruct((n_exp, n_blk, LANES), jnp.int32),
                   jax.ShapeDtypeStruct((n_exp, n_blk, LANES), jnp.float32)],
        compiler_params=_cparams(("arbitrary",)),
        name="route_compact",
    )(offs_flat, pos, aff)


def _route_set(aff, cap, tok_base):
    nc, n_exp, _ = aff.shape
    pos, off = _route(aff, cap)
    offs = jnp.concatenate([off[:, :, 0].T, jnp.full((n_exp, 1), cap, jnp.int32)], axis=1)
    idx, gate = _compact(offs.reshape(-1), pos, aff, cap, tok_base)
    return idx.reshape(n_exp, cap), gate.reshape(n_exp, cap)


MOE_ROWS_PER_SINGLE_STEP = 0


def _moe_schedule(tm, n_f):
    rest = n_f - 6
    assert n_f % 2 == 0 and rest > 0 and tm % 4 == 0 and (tm // 2) % rest == 0
    return ((4, (("scatter", tm // 4),)),
            (2, (("next", tm // 4),)),
            (rest, (("residual", tm // rest), ("next", tm // 2 // rest))))


def _moe_kernel(idx_prev, idx_cur, idx_nxt, gate_ref, wg_hbm, wu_hbm, wd_hbm, h2_hbm, x1_hbm,
                out_hbm, xe32, xe16, acc, orow, wg_buf, wu_buf, wd_buf, sems, wsems,
                *, tm, tf, nt, n_tiles, n_f):
    del x1_hbm
    n = pl.program_id(0)
    sem_x, sem_g, sem_s = sems.at[0], sems.at[1], sems.at[2]
    last_step = n_tiles * n_f - 1

    def weight_copies(step, slot):
        step = jnp.minimum(step, last_step)
        e = step // (nt * n_f)
        col = pl.multiple_of((step % n_f) * tf, tf)
        return (
            pltpu.make_async_copy(wg_hbm.at[e, :, pl.ds(col, tf)], wg_buf.at[slot], wsems.at[0, slot]),
            pltpu.make_async_copy(wu_hbm.at[e, :, pl.ds(col, tf)], wu_buf.at[slot], wsems.at[1, slot]),
            pltpu.make_async_copy(wd_hbm.at[e, pl.ds(col, tf), :], wd_buf.at[slot], wsems.at[2, slot]),
        )

    def xe_row(idx_smem, s):
        return pltpu.make_async_copy(h2_hbm.at[pl.ds(idx_smem[0, 0, s], 1)], xe32.at[pl.ds(s, 1)], sem_x)

    def residual_row(s):
        return pltpu.make_async_copy(out_hbm.at[pl.ds(idx_cur[0, 0, s], 1)], orow.at[pl.ds(s, 1)], sem_g)

    def scatter_row(idx_smem, s):
        return pltpu.make_async_copy(orow.at[pl.ds(s, 1)], out_hbm.at[pl.ds(idx_smem[0, 0, s], 1)], sem_s)

    def wait_xe():
        pltpu.make_async_copy(h2_hbm.at[pl.ds(0, tm)], xe32, sem_x).wait()

    def wait_scatter():
        pltpu.make_async_copy(orow, out_hbm.at[pl.ds(0, tm)], sem_s).wait()

    def ffn_step(f, slot=None):
        if slot is None:
            slot = f % 2
        for c in weight_copies(n * n_f + f, slot):
            c.wait()
        for c in weight_copies(n * n_f + f + 1, 1 - slot):
            c.start()
        x = xe16[...]
        gp = jnp.dot(x, wg_buf[slot].astype(jnp.bfloat16), preferred_element_type=jnp.float32)
        up = jnp.dot(x, wu_buf[slot].astype(jnp.bfloat16), preferred_element_type=jnp.float32)
        hid = (gp * jax.nn.sigmoid(gp) * up).astype(jnp.bfloat16)
        acc[...] += jnp.dot(hid, wd_buf[slot].astype(jnp.bfloat16),
                            preferred_element_type=jnp.float32)

    def run_steps(f_lo, count, row_work, per_iter=1):
        assert per_iter == 1 or (per_iter == 2 and f_lo % 2 == 0)

        def body(p, _):
            for u in range(per_iter):
                ffn_step(f_lo + per_iter * p + u, u if per_iter == 2 else None)
                row_work(per_iter * p + u)
            return 0
        lax.fori_loop(0, count // per_iter, body, 0)

    @pl.when(n == 0)
    def _():
        def body(s, _):
            xe_row(idx_cur, s).start()
            return 0
        lax.fori_loop(0, tm, body, 0, unroll=ROW_DMA_UNROLL)
        for c in weight_copies(0, 0):
            c.start()

    wait_xe()
    packed = xe32[...]
    half = packed.shape[1]
    xe16[:, :half] = lax.bitcast_convert_type(packed << 16, jnp.float32).astype(jnp.bfloat16)
    xe16[:, half:] = lax.bitcast_convert_type(
        packed & jnp.uint32(0xFFFF0000), jnp.float32).astype(jnp.bfloat16)
    acc[...] = jnp.zeros_like(acc)

    f_lo = 0
    issued = {"scatter": 0, "next": 0, "residual": 0}
    for steps, works in _moe_schedule(tm, n_f):
        kinds = [kind for kind, _ in works]

        def row_work(k, works=works, base=dict(issued)):
            for kind, rows in works:
                for r in range(rows):
                    s = base[kind] + k * rows + r
                    if kind == "scatter":
                        scatter_row(idx_prev, s).start()
                    elif kind == "next":
                        xe_row(idx_nxt, s).start()
                    else:
                        residual_row(s).start()

        if "residual" in kinds:
            assert issued["scatter"] == tm and issued["residual"] == 0
            pl.when(n > 0)(wait_scatter)
        per_iter = 2 if sum(rows for _, rows in works) > MOE_ROWS_PER_SINGLE_STEP else 1
        if "scatter" in kinds:
            assert len(kinds) == 1
            pl.when(n > 0)(functools.partial(run_steps, f_lo, steps, row_work, per_iter))
            pl.when(n == 0)(functools.partial(run_steps, f_lo, steps, lambda k: None))
        else:
            run_steps(f_lo, steps, row_work, per_iter)
        for kind, rows in works:
            issued[kind] += steps * rows
        f_lo += steps
    assert f_lo == n_f and all(v == tm for v in issued.values())

    pltpu.make_async_copy(out_hbm.at[pl.ds(0, tm)], orow, sem_g).wait()
    g_t = gate_ref[0].T
    for j in range(tm // LANES):
        rs = slice(j * LANES, (j + 1) * LANES)
        orow[rs, :] = orow[rs, :] + acc[rs, :] * g_t[:, j:j + 1]

    @pl.when(n == n_tiles - 1)
    def _():
        def body(s, _):
            scatter_row(idx_cur, s).start()
            return 0
        lax.fori_loop(0, tm, body, 0, unroll=ROW_DMA_UNROLL)
        wait_scatter()
        wait_xe()
        for c in weight_copies(last_step, 0):
            c.wait()


def _moe(idx, gate, h2, x1, wg, wu, wd, tm, tf):
    n_exp, slots = idx.shape
    ntok, d = x1.shape
    ff = wg.shape[2]
    nt = slots // tm
    n_f = ff // tf
    n_tiles = n_exp * nt
    _moe_schedule(tm, n_f)
    idx3 = idx.reshape(n_tiles, 1, tm)
    gate3 = gate.reshape(n_exp, slots // LANES, LANES)
    smem = pltpu.MemorySpace.SMEM
    hbm = pl.BlockSpec(memory_space=pl.ANY)
    in_specs = [
        pl.BlockSpec((1, 1, tm), lambda n: (jnp.maximum(n - 1, 0), 0, 0), memory_space=smem),
        pl.BlockSpec((1, 1, tm), lambda n: (n, 0, 0), memory_space=smem),
        pl.BlockSpec((1, 1, tm), lambda n: (jnp.minimum(n + 1, n_tiles - 1), 0, 0), memory_space=smem),
        pl.BlockSpec((1, tm // LANES, LANES), lambda n: (n // nt, n % nt, 0)),
        hbm, hbm, hbm, hbm, hbm,
    ]
    return pl.pallas_call(
        functools.partial(_moe_kernel, tm=tm, tf=tf, nt=nt, n_tiles=n_tiles, n_f=n_f),
        grid=(n_tiles,), in_specs=in_specs, out_specs=hbm,
        out_shape=jax.ShapeDtypeStruct((ntok, d), jnp.float32),
        scratch_shapes=[
            pltpu.VMEM((tm, d // 2), jnp.uint32),
            pltpu.VMEM((tm, d), jnp.bfloat16),
            pltpu.VMEM((tm, d), jnp.float32),
            pltpu.VMEM((tm, d), jnp.float32),
            pltpu.VMEM((2, d, tf), jnp.float32),
            pltpu.VMEM((2, d, tf), jnp.float32),
            pltpu.VMEM((2, tf, d), jnp.float32),
            pltpu.SemaphoreType.DMA((3,)),
            pltpu.SemaphoreType.DMA((3, 2)),
        ],
        input_output_aliases={8: 0},
        compiler_params=_cparams(("arbitrary",)),
        name="expert_ffn",
    )(idx3, idx3, idx3, gate3, wg, wu, wd, h2, x1)


def _pick(n, candidates):
    for c in candidates:
        if n % c == 0:
            return c
    raise ValueError(f"no tile among {candidates} divides {n}")


def _layer(x_p, x_s, seq_p, seq_s, rel_table, norm_mix_g, w_in, q_norm_a, k_norm_a, q_norm_b,
           k_norm_b, sink_b, w_proj_a, w_proj_b, w_out, norm_ffn_g, w_router, w_gate_e, w_up_e,
           w_down_e):
    np_tok, d = x_p.shape
    ns_tok = x_s.shape[0]
    ntok = np_tok + ns_tok
    bf = jnp.bfloat16
    tm_in = _pick(math.gcd(np_tok, ns_tok), (1024, 512, 256))

    def group_cols(g):
        return [w_in[:, s * A_QKV + g * GROUP_COLS: s * A_QKV + (g + 1) * GROUP_COLS] for s in range(3)]

    w_nat = jnp.concatenate(group_cols(0) + [w_in[:, 3 * A_QKV:]], axis=1).astype(bf)
    z = _inproj(x_p, x_s, norm_mix_g, w_nat, tm_in, _pick(w_nat.shape[1], (1024, 512)))

    heads_a = tuple((h * HEAD_DIM, GROUP_COLS + h * HEAD_DIM, 2 * GROUP_COLS + h * HEAD_DIM)
                    for h in range(A_HEADS))
    heads_b = tuple((h * HEAD_DIM, B_Q + (h // B_GROUP) * HEAD_DIM,
                     B_Q + B_KV + (h // B_GROUP) * HEAD_DIM) for h in range(B_Q_HEADS))
    oas, lses = [], []
    for g in range(N_DIL_GROUPS):
        dil = DIL_RATES[g]
        radius = (DIL_WINDOWS[g] // 2) // dil
        tq = min(ATTN_TQ, seq_p // dil, seq_s // dil)
        bias = _bias_tile(rel_table[:, g * A_HEADS:(g + 1) * A_HEADS], dil, radius,
                          min(ATTN_SUB, tq))
        if dil == 1:
            zv = z
        else:
            w_g = jnp.concatenate(group_cols(g), axis=1).astype(bf)
            zv = _inproj(x_p, x_s, norm_mix_g, w_g, tm_in, w_g.shape[1], dil=dil)
        o, lse = _banded_attention(
            zv, dil=dil, col0=0, radius=radius, tq=tq, heads=heads_a,
            res_per_step=ATTN_RES_PER_STEP if (dil > 1 and tq <= ATTN_SUB) else 1, bias=bias,
            q_w=q_norm_a, k_w=k_norm_a, sink=None, with_lse=True, np_tok=np_tok, seq_p=seq_p,
            seq_s=seq_s)
        oas.append(o)
        lses.append(lse)
    tq_b = min(ATTN_TQ, seq_p, seq_s)
    bias_b = _bias_tile(rel_table[:, N_DIL_GROUPS * A_HEADS:], 1, B_RADIUS, min(ATTN_SUB_B, tq_b))
    ob = _banded_attention(
        z, dil=1, col0=QKV_COLS, radius=B_RADIUS, tq=tq_b, heads=heads_b, res_per_step=1,
        bias=bias_b, q_w=q_norm_b, k_w=k_norm_b, sink=sink_b, with_lse=False, np_tok=np_tok,
        seq_p=seq_p, seq_s=seq_s)

    x1_p, x1_s, h2_p, h2_s, aff = _merge(
        x_p, x_s, oas, lses, ob, z, w_proj_a.astype(bf), w_proj_b.astype(bf), w_out.astype(bf),
        norm_ffn_g, w_router.T.astype(bf), _pick(math.gcd(np_tok, ns_tok), (256,)))

    n_exp = w_router.shape[1]
    tf = _pick(w_gate_e.shape[2], (256, 128))
    outs = []
    for x1, h2, aff_set in ((x1_p, h2_p, aff[:np_tok // LANES]), (x1_s, h2_s, aff[np_tok // LANES:])):
        cap = max(1, EC_CAPACITY * x1.shape[0] // n_exp)
        idx, gate = _route_set(aff_set, cap, 0)
        outs.append(_moe(idx, gate, h2, x1, w_gate_e, w_up_e, w_down_e, _pick(cap, (1024,)), tf))
    return outs


def kernel(x_prompt, x_sample, rel_table, norm_mix_g, w_in, q_norm_a, k_norm_a, q_norm_b, k_norm_b,
           sink_b, w_proj_a, w_proj_b, w_out, norm_ffn_g, w_router, w_gate_e, w_up_e, w_down_e):
    bp, sp, d = x_prompt.shape
    bs, ss, _ = x_sample.shape
    x_p = x_prompt.reshape(bp * sp, d)
    x_s = x_sample.reshape(bs * ss, d)
    for l in range(norm_mix_g.shape[0]):
        x_p, x_s = _layer(x_p, x_s, sp, ss, rel_table, norm_mix_g[l], w_in[l], q_norm_a[l],
                          k_norm_a[l], q_norm_b[l], k_norm_b[l], sink_b[l], w_proj_a[l],
                          w_proj_b[l], w_out[l], norm_ffn_g[l], w_router[l], w_gate_e[l],
                          w_up_e[l], w_down_e[l])
    return x_p.reshape(bp, sp, d), x_s.reshape(bs, ss, d)
```

```python
import functools
import math

import jax
import jax.numpy as jnp
from jax import lax
from jax.experimental import pallas as pl
from jax.experimental.pallas import tpu as pltpu

HEAD_DIM = 128
DIL_WINDOWS = (128, 512, 2048)
DIL_RATES = (1, 4, 16)
N_DIL_GROUPS = 3
A_HEADS = 4
B_Q_HEADS = 8
B_KV_HEADS = 2
B_GROUP = B_Q_HEADS // B_KV_HEADS
B_RADIUS = 128
REL_BUCKETS = 32
REL_MAX_DIST = 1024
EC_CAPACITY = 2
NORM_EPS = 1e-6
NEG_INF = -1e30

A_QKV = N_DIL_GROUPS * A_HEADS * HEAD_DIM
A_OUT = A_HEADS * HEAD_DIM
B_Q = B_Q_HEADS * HEAD_DIM
B_KV = B_KV_HEADS * HEAD_DIM

LANES = 128
GROUP_COLS = A_HEADS * HEAD_DIM
VMEM_LIMIT = 60 * 1024 * 1024
INPROJ_CHUNK = 256
QKV_COLS = 3 * GROUP_COLS
ATTN_TQ = 512
ATTN_SUB = 128
ATTN_SUB_B = 128
ATTN_RES_PER_STEP = 4
COMPACT_CHUNKS = 8
ROW_DMA_UNROLL = 8


def _cparams(sem, vmem=VMEM_LIMIT):
    return pltpu.CompilerParams(dimension_semantics=sem, vmem_limit_bytes=vmem)


def _inproj_kernel(xp_ref, xs_ref, g_ref, w_ref, z_ref, h_scr, *, tm, n_p):
    def normalise(x_ref):
        x = x_ref[...]
        ms = jnp.mean(x * x, axis=-1, keepdims=True)
        h_scr[...] = (x * lax.rsqrt(ms + NORM_EPS) * g_ref[...]).astype(jnp.bfloat16)

    first_col = pl.program_id(1) == 0
    is_prompt = pl.program_id(0) < n_p
    pl.when(jnp.logical_and(first_col, is_prompt))(lambda: normalise(xp_ref))
    pl.when(jnp.logical_and(first_col, jnp.logical_not(is_prompt)))(lambda: normalise(xs_ref))

    z_ref[...] = jnp.dot(h_scr[...], w_ref[...],
                         preferred_element_type=jnp.float32).astype(jnp.bfloat16)


def _inproj_strided_kernel(xp_ref, xs_ref, g_ref, w_ref, z_ref, zs, *, dil, tm, n_p):
    tn = w_ref.shape[1]
    chunk = INPROJ_CHUNK
    rows = chunk // dil

    def run(x_ref):
        for ci in range(tm // chunk):
            x = x_ref[ci * chunk:(ci + 1) * chunk, :]
            ms = jnp.mean(x * x, axis=-1, keepdims=True)
            h = (x * lax.rsqrt(ms + NORM_EPS) * g_ref[...]).astype(jnp.bfloat16)
            z = jnp.dot(h, w_ref[...], preferred_element_type=jnp.float32)
            slab = zs.at[ci % 2]
            for c in range(tn // LANES):
                slab[c] = z[:, c * LANES:(c + 1) * LANES]
            for r in range(dil):
                for c in range(tn // LANES):
                    z_ref[ci * rows:(ci + 1) * rows, r * tn + c * LANES:r * tn + (c + 1) * LANES] = (
                        slab[c, pl.ds(r, rows, stride=dil), :].astype(jnp.bfloat16))

    is_prompt = pl.program_id(0) < n_p
    pl.when(is_prompt)(lambda: run(xp_ref))
    pl.when(jnp.logical_not(is_prompt))(lambda: run(xs_ref))


def _inproj(x_p, x_s, g, w_bf16, tm, tn, dil=1):
    d = x_p.shape[1]
    n_p = x_p.shape[0] // tm
    n = x_p.shape[0] + x_s.shape[0]
    cols = w_bf16.shape[1]
    assert dil == 1 or tn == cols
    if dil == 1:
        kern = functools.partial(_inproj_kernel, tm=tm, n_p=n_p)
        scratch = [pltpu.VMEM((tm, d), jnp.bfloat16)]
    else:
        kern = functools.partial(_inproj_strided_kernel, dil=dil, tm=tm, n_p=n_p)
        scratch = [pltpu.VMEM((2, tn // LANES, INPROJ_CHUNK, LANES), jnp.float32)]
    return pl.pallas_call(
        kern,
        grid=(n // tm, cols // tn),
        in_specs=[
            pl.BlockSpec((tm, d), lambda i, j: (jnp.minimum(i, n_p - 1), 0)),
            pl.BlockSpec((tm, d), lambda i, j: (jnp.maximum(i - n_p, 0), 0)),
            pl.BlockSpec((1, d), lambda i, j: (0, 0)),
            pl.BlockSpec((d, tn), lambda i, j: (0, j)),
        ],
        out_specs=pl.BlockSpec((tm // dil, dil * tn), lambda i, j: (i, j)),
        out_shape=jax.ShapeDtypeStruct((n // dil, dil * cols), jnp.bfloat16),
        scratch_shapes=scratch,
        compiler_params=_cparams(("parallel", "arbitrary")),
        name=f"inproj_d{dil}",
    )(x_p, x_s, g.reshape(1, d), w_bf16)


def _t5_bucket(rel):
    half = REL_BUCKETS // 2
    max_exact = half // 2
    n = jnp.abs(rel)
    base = jnp.where(rel > 0, half, 0)
    nf = jnp.maximum(n, 1).astype(jnp.float32)
    large = max_exact + (jnp.log(nf / max_exact) / math.log(REL_MAX_DIST / max_exact)
                         * (half - max_exact)).astype(jnp.int32)
    large = jnp.minimum(large, half - 1)
    return base + jnp.where(n < max_exact, n, large)


def _bias_tile(table_cols, dil, radius, tq):
    tk = tq + 2 * radius
    n_heads = table_cols.shape[1]
    rel = jnp.arange(-radius, radius + 1)
    vals = table_cols[_t5_bucket(rel * dil)].astype(jnp.float32).T
    period = tq + tk
    w = jnp.full((n_heads, period), NEG_INF, jnp.float32).at[:, :2 * radius + 1].set(vals)
    flat = jnp.tile(w, (1, tq))[:, :tq * (period - 1)]
    return flat.reshape(n_heads, tq, period - 1)[:, :, :tk]


def _head_norm(x, w):
    xf = x.astype(jnp.float32)
    ms = jnp.mean(xf * xf, axis=-1, keepdims=True)
    return xf * lax.rsqrt(ms + NORM_EPS) * w


def _attn_kernel(*refs, tq, sub, radius, heads, res_per_step, with_sink, with_lse, np_rows,
                 len_p, len_s):
    cur_ref, prev_ref, next_ref, bias_ref, qw_ref, kw_ref = refs[:6]
    pos = 6
    sink_ref = None
    if with_sink:
        sink_ref = refs[pos]
        pos += 1
    o_ref = refs[pos]
    lse_ref = refs[pos + 1] if with_lse else None

    win = sub + 2 * radius
    n_heads = len(heads)
    q0 = pl.program_id(1) * tq
    lo_p = (q0 // len_p) * len_p
    lo_s = np_rows + ((q0 - np_rows) // len_s) * len_s
    in_p = q0 < np_rows
    lo = jnp.where(in_p, lo_p, lo_s)
    hi = lo + jnp.where(in_p, len_p, len_s)
    key_iota = lax.broadcasted_iota(jnp.int32, (1, win), 1)
    lane = lax.broadcasted_iota(jnp.int32, (sub, LANES), 1)
    scale = HEAD_DIM ** -0.5
    qw = qw_ref[...]
    kw = kw_ref[...]

    def window(col):
        cs = slice(col, col + HEAD_DIM)
        return jnp.concatenate([prev_ref[:, cs], cur_ref[:, cs], next_ref[:, cs]], axis=0)

    kv_groups = {}
    for h, (_, k_off, v_off) in enumerate(heads):
        kv_groups.setdefault((k_off, v_off), []).append(h)

    def stack(pieces):
        return pieces[0] if len(pieces) == 1 else jnp.concatenate(pieces, axis=0)

    for rr in range(res_per_step):
        base = rr * QKV_COLS
        lse_tiles = [jnp.zeros((sub, LANES), jnp.float32) for _ in range(tq // sub)]
        for (k_off, v_off), hs in kv_groups.items():
            kh = _head_norm(window(base + k_off), kw).astype(jnp.bfloat16)
            vh = window(base + v_off)
            qs = [(_head_norm(cur_ref[:, base + heads[h][0]:base + heads[h][0] + HEAD_DIM], qw)
                   * scale).astype(jnp.bfloat16) for h in hs]
            bias = stack([bias_ref[h] for h in hs])
            if with_sink:
                sink = stack([sink_ref[h] for h in hs])
                is_sink = sink > NEG_INF
            for a in range(tq // sub):
                kpos = q0 + a * sub - radius + key_iota
                valid = (kpos >= lo) & (kpos < hi)
                s = lax.dot_general(stack([q[a * sub:(a + 1) * sub] for q in qs]),
                                    kh[a * sub:a * sub + win], (((1,), (1,)), ((), ())),
                                    preferred_element_type=jnp.float32)
                s = jnp.where(valid, s + bias, NEG_INF)
                if with_sink:
                    s = jnp.maximum(s, sink)
                m = jnp.max(s, axis=-1, keepdims=True)
                p = jnp.exp(s - m)
                l = jnp.sum(p, axis=-1, keepdims=True)
                if with_sink:
                    p = jnp.where(is_sink, 0.0, p)
                o = jnp.dot(p.astype(jnp.bfloat16), vh[a * sub:a * sub + win],
                            preferred_element_type=jnp.float32) / l
                lse = m + jnp.log(l)
                for gi, h in enumerate(hs):
                    oc = (rr * n_heads + h) * HEAD_DIM
                    o_ref[a * sub:(a + 1) * sub, oc:oc + HEAD_DIM] = (
                        o[gi * sub:(gi + 1) * sub].astype(o_ref.dtype))
                    if with_lse:
                        lse_tiles[a] = jnp.where(lane == h, lse[gi * sub:(gi + 1) * sub],
                                                 lse_tiles[a])
        if with_lse:
            for a in range(tq // sub):
                lse_ref[a * sub:(a + 1) * sub, rr * LANES:(rr + 1) * LANES] = lse_tiles[a]


def _banded_attention(zv, *, dil, col0, radius, tq, heads, res_per_step, bias, q_w, k_w, sink,
                      with_lse, np_tok, seq_p, seq_s):
    rows = zv.shape[0]
    ntok = rows * dil
    n_heads = len(heads)
    assert col0 % QKV_COLS == 0 and dil % res_per_step == 0
    assert dil == 1 or zv.shape[1] == dil * QKV_COLS
    blk0 = col0 // QKV_COLS
    nq = rows // tq
    hb = tq // radius
    n_halo = rows // radius
    sub = bias.shape[1]
    wblk = res_per_step * QKV_COLS
    const2 = lambda r, i: (0, 0)
    const3 = lambda r, i: (0, 0, 0)
    in_specs = [
        pl.BlockSpec((tq, wblk), lambda r, i: (i, blk0 + r)),
        pl.BlockSpec((radius, wblk), lambda r, i: (jnp.maximum(i * hb - 1, 0), blk0 + r)),
        pl.BlockSpec((radius, wblk), lambda r, i: (jnp.minimum((i + 1) * hb, n_halo - 1), blk0 + r)),
        pl.BlockSpec((n_heads, sub, sub + 2 * radius), const3),
        pl.BlockSpec((1, HEAD_DIM), const2),
        pl.BlockSpec((1, HEAD_DIM), const2),
    ]
    args = [zv, zv, zv, bias, q_w.reshape(1, HEAD_DIM), k_w.reshape(1, HEAD_DIM)]
    if sink is not None:
        win = sub + 2 * radius
        row = jnp.arange(sub)
        col = jnp.where(row + 2 * radius + 1 < win, row + 2 * radius + 1, row - 1)
        at_col = jnp.arange(win)[None, :] == col[:, None]
        in_specs.append(pl.BlockSpec((n_heads, sub, win), const3))
        args.append(jnp.where(at_col[None], sink.astype(jnp.float32)[:, None, None], NEG_INF))
    omap = lambda r, i: (i, r)
    out_specs = [pl.BlockSpec((tq, res_per_step * n_heads * HEAD_DIM), omap)]
    out_shape = [jax.ShapeDtypeStruct((rows, dil * n_heads * HEAD_DIM), jnp.bfloat16)]
    if with_lse:
        out_specs.append(pl.BlockSpec((tq, res_per_step * LANES), omap))
        out_shape.append(jax.ShapeDtypeStruct((rows, dil * LANES), jnp.float32))
    kern = functools.partial(
        _attn_kernel, tq=tq, sub=sub, radius=radius, heads=heads, res_per_step=res_per_step,
        with_sink=sink is not None, with_lse=with_lse, np_rows=np_tok // dil,
        len_p=seq_p // dil, len_s=seq_s // dil)
    outs = pl.pallas_call(
        kern, grid=(dil // res_per_step, nq), in_specs=in_specs, out_specs=out_specs,
        out_shape=out_shape, compiler_params=_cparams(("parallel", "arbitrary")),
        name=f"band_attn_d{dil}_r{radius}",
    )(*args)
    return tuple(outs) if with_lse else outs[0]


def _merge_kernel(xp_ref, xs_ref, oa0_ref, oa1_ref, oa2_ref, l0_ref, l1_ref, l2_ref, ob_ref, ga0_ref,
                  ga1_ref, gb0_ref, gb1_ref, wpa_ref, wpb_ref, wo_ref, g2_ref, wr_ref, x1p_ref,
                  x1s_ref, h2p_ref, h2s_ref, aff_ref, oa_scr, l_scr, merged_scr, *, tm, n_p):
    step = pl.program_id(0)
    is_prompt = jnp.maximum(step - 1, 0) < n_p

    @pl.when(step == 0)
    def _():
        merged_scr[...] = jnp.zeros_like(merged_scr)

    delta = jnp.dot(merged_scr[...], wo_ref[...], preferred_element_type=jnp.float32)
    ga = jnp.concatenate([ga0_ref[...], ga1_ref[...]], axis=1)
    gb = jnp.concatenate([gb0_ref[...], gb1_ref[...]], axis=1)

    def natural_order(g, o_ref, l_ref):
        dil = DIL_RATES[g]
        if dil == 1:
            return ([o_ref[:, h * HEAD_DIM:(h + 1) * HEAD_DIM].astype(jnp.float32)
                     for h in range(A_HEADS)], l_ref[...])
        rows = tm // dil
        for r in range(dil):
            for h in range(A_HEADS):
                c0 = r * A_OUT + h * HEAD_DIM
                oa_scr[g, h, pl.ds(r, rows, stride=dil), :] = (
                    o_ref[:, c0:c0 + HEAD_DIM].astype(jnp.float32))
            l_scr[g, pl.ds(r, rows, stride=dil), :] = l_ref[:, r * LANES:(r + 1) * LANES]
        return [oa_scr[g, h] for h in range(A_HEADS)], l_scr[g]

    (o0, l0), (o1, l1), (o2, l2) = (natural_order(0, oa0_ref, l0_ref),
                                    natural_order(1, oa1_ref, l1_ref),
                                    natural_order(2, oa2_ref, l2_ref))
    mx = jnp.maximum(jnp.maximum(l0, l1), l2)
    e0, e1, e2 = jnp.exp(l0 - mx), jnp.exp(l1 - mx), jnp.exp(l2 - mx)
    den = e0 + e1 + e2
    w0, w1, w2 = e0 / den, e1 / den, e2 / den
    parts = [w0[:, h:h + 1] * o0[h] + w1[:, h:h + 1] * o1[h] + w2[:, h:h + 1] * o2[h]
             for h in range(A_HEADS)]
    o_a = jnp.concatenate(parts, axis=1).astype(jnp.bfloat16)
    pa = jnp.dot(o_a, wpa_ref[...], preferred_element_type=jnp.float32)
    pb = jnp.dot(ob_ref[...], wpb_ref[...], preferred_element_type=jnp.float32)
    merged = (jax.nn.sigmoid(ga.astype(jnp.float32)) * pa
              + jax.nn.sigmoid(gb.astype(jnp.float32)) * pb)
    merged_scr[...] = merged.astype(jnp.bfloat16)

    def finish(x_ref, x1_ref, h2_ref):
        x1 = x_ref[...] + delta
        x1_ref[...] = x1
        ms = jnp.mean(x1 * x1, axis=-1, keepdims=True)
        h2 = (x1 * lax.rsqrt(ms + NORM_EPS) * g2_ref[...]).astype(jnp.bfloat16)
        bits = lax.bitcast_convert_type(h2.astype(jnp.float32), jnp.uint32)
        half = bits.shape[1] // 2
        h2_ref[...] = (bits[:, half:] & jnp.uint32(0xFFFF0000)) | (bits[:, :half] >> 16)
        logits = lax.dot_general(wr_ref[...], h2, (((1,), (1,)), ((), ())),
                                 preferred_element_type=jnp.float32)
        mx2 = jnp.max(logits, axis=0, keepdims=True)
        ex = jnp.exp(logits - mx2)
        aff = ex / jnp.sum(ex, axis=0, keepdims=True)
        for j in range(tm // LANES):
            aff_ref[j] = aff[:, j * LANES:(j + 1) * LANES]

    pl.when(is_prompt)(lambda: finish(xp_ref, x1p_ref, h2p_ref))
    pl.when(jnp.logical_not(is_prompt))(lambda: finish(xs_ref, x1s_ref, h2s_ref))


def _merge(x_p, x_s, oas, lses, ob, z, wpa, wpb, wo, g2, wr_t, tm):
    d = x_p.shape[1]
    n_p = x_p.shape[0] // tm
    n = x_p.shape[0] + x_s.shape[0]
    n_exp = wr_t.shape[0]
    half = d // 2
    assert (2 * QKV_COLS) % half == 0
    ga_blk = 2 * QKV_COLS // half
    n_tiles = n // tm
    formed = lambda i: jnp.minimum(i, n_tiles - 1)
    finished = lambda i: jnp.maximum(i - 1, 0)
    gate = lambda k: pl.BlockSpec((tm, half), lambda i: (formed(i), ga_blk + k))
    row = lambda i: (formed(i), 0)
    const = lambda i: (0, 0)
    row_p = lambda i: (jnp.minimum(finished(i), n_p - 1), 0)
    row_s = lambda i: (jnp.maximum(finished(i) - n_p, 0), 0)
    in_specs = [
        pl.BlockSpec((tm, d), row_p),
        pl.BlockSpec((tm, d), row_s),
        *[pl.BlockSpec((tm // dl, dl * A_OUT), row) for dl in DIL_RATES],
        *[pl.BlockSpec((tm // dl, dl * LANES), row) for dl in DIL_RATES],
        pl.BlockSpec((tm, B_Q), row),
        gate(0), gate(1), gate(2), gate(3),
        pl.BlockSpec((A_OUT, d), const),
        pl.BlockSpec((B_Q, d), const),
        pl.BlockSpec((d, d), const),
        pl.BlockSpec((1, d), const),
        pl.BlockSpec((n_exp, d), const),
    ]
    out_specs = [
        pl.BlockSpec((tm, d), row_p),
        pl.BlockSpec((tm, d), row_s),
        pl.BlockSpec((tm, half), row_p),
        pl.BlockSpec((tm, half), row_s),
        pl.BlockSpec((tm // LANES, n_exp, LANES), lambda i: (finished(i), 0, 0)),
    ]
    out_shape = [
        jax.ShapeDtypeStruct(x_p.shape, jnp.float32),
        jax.ShapeDtypeStruct(x_s.shape, jnp.float32),
        jax.ShapeDtypeStruct((x_p.shape[0], half), jnp.uint32),
        jax.ShapeDtypeStruct((x_s.shape[0], half), jnp.uint32),
        jax.ShapeDtypeStruct((n // LANES, n_exp, LANES), jnp.float32),
    ]
    return pl.pallas_call(
        functools.partial(_merge_kernel, tm=tm, n_p=n_p),
        grid=(n_tiles + 1,), in_specs=in_specs, out_specs=out_specs, out_shape=out_shape,
        scratch_shapes=[pltpu.VMEM((N_DIL_GROUPS, A_HEADS, tm, LANES), jnp.float32),
                        pltpu.VMEM((N_DIL_GROUPS, tm, LANES), jnp.float32),
                        pltpu.VMEM((tm, d), jnp.bfloat16)],
        compiler_params=_cparams(("arbitrary",)),
        name="merge_proj_router",
    )(x_p, x_s, oas[0], oas[1], oas[2], lses[0], lses[1], lses[2], ob, z, z, z, z, wpa, wpb, wo,
      g2.reshape(1, d), wr_t)


def _prefix_counts(flag_f32, tri, tot_scr, off_scr, nc):
    n_exp = flag_f32.shape[1]
    incl = jnp.dot(flag_f32.astype(jnp.bfloat16).reshape(nc * n_exp, LANES), tri,
                   preferred_element_type=jnp.float32).reshape(nc, n_exp, LANES)
    tot_scr[...] = jnp.broadcast_to(incl[:, :, LANES - 1:LANES], (nc, n_exp, LANES))

    def body(c, run):
        off_scr[c] = run
        return run + tot_scr[c]

    lax.fori_loop(0, nc, body, jnp.zeros((n_exp, LANES), jnp.float32))
    return off_scr[...] + incl - flag_f32


def _route_kernel(aff_ref, pos_ref, off_ref, tot_scr, off_scr, *, cap, nc):
    n_exp = aff_ref.shape[1]
    capf = jnp.float32(cap)

    def count(mask):
        c = jnp.sum(jnp.where(mask, 1.0, 0.0), axis=0)
        return jnp.sum(c, axis=-1, keepdims=True)

    def bit_body(k, t):
        cand = t | jnp.left_shift(jnp.int32(1), 30 - k)
        bits = pltpu.bitcast(aff_ref[...], jnp.int32)
        return jnp.where(count(bits >= cand[None]) >= capf, cand, t)

    t = lax.fori_loop(0, 31, bit_body, jnp.zeros((n_exp, 1), jnp.int32))
    bits = pltpu.bitcast(aff_ref[...], jnp.int32)
    gt = bits > t[None]
    eq = bits == t[None]
    need = capf - count(gt)
    rows = lax.broadcasted_iota(jnp.int32, (LANES, LANES), 0)
    cols = lax.broadcasted_iota(jnp.int32, (LANES, LANES), 1)
    tri = jnp.where(rows <= cols, 1.0, 0.0).astype(jnp.bfloat16)
    eq_f = jnp.where(eq, 1.0, 0.0)
    tie_rank = _prefix_counts(eq_f, tri, tot_scr, off_scr, nc)
    sel = gt | (eq & (tie_rank < need[None]))
    sel_f = jnp.where(sel, 1.0, 0.0)
    slot = _prefix_counts(sel_f, tri, tot_scr, off_scr, nc)
    pos_ref[...] = jnp.where(sel, slot, -1.0).astype(jnp.int32)
    off_ref[...] = off_scr[...].astype(jnp.int32)


def _route(aff, cap):
    nc, n_exp, _ = aff.shape
    full = pl.BlockSpec((nc, n_exp, LANES), lambda: (0, 0, 0))
    return pl.pallas_call(
        functools.partial(_route_kernel, cap=cap, nc=nc),
        in_specs=[full], out_specs=[full, full],
        out_shape=[jax.ShapeDtypeStruct((nc, n_exp, LANES), jnp.int32)] * 2,
        scratch_shapes=[pltpu.VMEM((nc, n_exp, LANES), jnp.float32)] * 2,
        compiler_params=pltpu.CompilerParams(vmem_limit_bytes=VMEM_LIMIT),
        name="route_select",
    )(aff)


def _split3(x):
    a = x.astype(jnp.bfloat16).astype(jnp.float32)
    r = x - a
    b = r.astype(jnp.bfloat16).astype(jnp.float32)
    c = r - b
    return a, b, c


def _compact_kernel(off_smem, pos_ref, aff_ref, idx_ref, gate_ref, acc_scr, *, nc, n_blk, tok_base):
    e = pl.program_id(0)
    stride = nc + 1
    sub = lax.broadcasted_iota(jnp.int32, (LANES, LANES), 0)
    sub16 = lax.broadcasted_iota(jnp.int32, (16, LANES), 0)
    lane = lax.broadcasted_iota(jnp.int32, (1, LANES), 1)

    n_grp = nc // COMPACT_CHUNKS

    def chunk_off(c):
        return off_smem[e * stride + jnp.minimum(c, nc)]

    def block_body(sb, g_first):
        lo_slot = sb * LANES

        def skip_cond(g):
            return jnp.logical_and(g < n_grp, chunk_off((g + 1) * COMPACT_CHUNKS) <= lo_slot)

        g_first = lax.while_loop(skip_cond, lambda g: g + 1, g_first)
        acc_scr[...] = jnp.zeros_like(acc_scr)

        def take_cond(g):
            return jnp.logical_and(g < n_grp, chunk_off(g * COMPACT_CHUNKS) < lo_slot + LANES)

        def take(g):
            total = jnp.zeros((16, LANES), jnp.float32)
            for k in range(COMPACT_CHUNKS):
                c = g * COMPACT_CHUNKS + k
                rel = pos_ref[c, pl.ds(e, 1), :] - lo_slot
                onehot = jnp.where(sub == rel, 1.0, 0.0).astype(jnp.bfloat16)
                tok = tok_base + c * LANES + lane
                g1, g2, g3 = _split3(aff_ref[c, pl.ds(e, 1), :])
                lhs = jnp.where(sub16 == 0, (tok >> 8).astype(jnp.float32),
                      jnp.where(sub16 == 1, (tok & 255).astype(jnp.float32),
                      jnp.where(sub16 == 2, g1,
                      jnp.where(sub16 == 3, g2,
                      jnp.where(sub16 == 4, g3, 0.0))))).astype(jnp.bfloat16)
                total = total + lax.dot_general(lhs, onehot, (((1,), (1,)), ((), ())),
                                                preferred_element_type=jnp.float32)
            acc_scr[...] += total
            return g + 1

        lax.while_loop(take_cond, take, g_first)
        acc = acc_scr[...]
        idx_ref[0, pl.ds(sb, 1), :] = (acc[0:1] * 256.0 + acc[1:2]).astype(jnp.int32)
        gate_ref[0, pl.ds(sb, 1), :] = (acc[2:3] + acc[3:4]) + acc[4:5]
        return g_first

    lax.fori_loop(0, n_blk, block_body, jnp.int32(0))


def _compact(offs_flat, pos, aff, cap, tok_base):
    nc, n_exp, _ = pos.shape
    n_blk = cap // LANES
    full = pl.BlockSpec((nc, n_exp, LANES), lambda e, off: (0, 0, 0))
    out = pl.BlockSpec((1, n_blk, LANES), lambda e, off: (e, 0, 0))
    return pl.pallas_call(
        functools.partial(_compact_kernel, nc=nc, n_blk=n_blk, tok_base=tok_base),
        grid_spec=pltpu.PrefetchScalarGridSpec(
            num_scalar_prefetch=1, grid=(n_exp,), in_specs=[full, full], out_specs=[out, out],
            scratch_shapes=[pltpu.VMEM((16, LANES), jnp.float32)]),
        out_shape=[jax.ShapeDtypeStruct((n_exp, n_blk, LANES), jnp.int32),
                   jax.ShapeDtypeStruct((n_exp, n_blk, LANES), jnp.float32)],
        compiler_params=_cparams(("arbitrary",)),
        name="route_compact",
    )(offs_flat, pos, aff)


def _route_set(aff, cap, tok_base):
    nc, n_exp, _ = aff.shape
    pos, off = _route(aff, cap)
    offs = jnp.concatenate([off[:, :, 0].T, jnp.full((n_exp, 1), cap, jnp.int32)], axis=1)
    idx, gate = _compact(offs.reshape(-1), pos, aff, cap, tok_base)
    return idx.reshape(n_exp, cap), gate.reshape(n_exp, cap)


WEIGHT_DMA_PRIORITY = 1
MOE_ROWS_PER_SINGLE_STEP = 0


def _moe_schedule(tm, n_f):
    rest = n_f - 6
    assert n_f % 2 == 0 and rest > 0 and tm % 4 == 0 and (tm // 2) % rest == 0
    return ((4, (("scatter", tm // 4),)),
            (2, (("next", tm // 4),)),
            (rest, (("residual", tm // rest), ("next", tm // 2 // rest))))


def _moe_kernel(idx_prev, idx_cur, idx_nxt, gate_ref, wg_hbm, wu_hbm, wd_hbm, h2_hbm, x1_hbm,
                out_hbm, xe32, xe16, acc, orow, wg_buf, wu_buf, wd_buf, sems, wsems,
                *, tm, tf, nt, n_tiles, n_f):
    del x1_hbm
    n = pl.program_id(0)
    sem_x, sem_g, sem_s = sems.at[0], sems.at[1], sems.at[2]
    last_step = n_tiles * n_f - 1

    def weight_copies(step, slot):
        step = jnp.minimum(step, last_step)
        e = step // (nt * n_f)
        col = pl.multiple_of((step % n_f) * tf, tf)
        return (
            pltpu.make_async_copy(wg_hbm.at[e, :, pl.ds(col, tf)], wg_buf.at[slot], wsems.at[0, slot]),
            pltpu.make_async_copy(wu_hbm.at[e, :, pl.ds(col, tf)], wu_buf.at[slot], wsems.at[1, slot]),
            pltpu.make_async_copy(wd_hbm.at[e, pl.ds(col, tf), :], wd_buf.at[slot], wsems.at[2, slot]),
        )

    def xe_row(idx_smem, s):
        return pltpu.make_async_copy(h2_hbm.at[pl.ds(idx_smem[0, 0, s], 1)], xe32.at[pl.ds(s, 1)], sem_x)

    def residual_row(s):
        return pltpu.make_async_copy(out_hbm.at[pl.ds(idx_cur[0, 0, s], 1)], orow.at[pl.ds(s, 1)], sem_g)

    def scatter_row(idx_smem, s):
        return pltpu.make_async_copy(orow.at[pl.ds(s, 1)], out_hbm.at[pl.ds(idx_smem[0, 0, s], 1)], sem_s)

    def wait_xe():
        pltpu.make_async_copy(h2_hbm.at[pl.ds(0, tm)], xe32, sem_x).wait()

    def wait_scatter():
        pltpu.make_async_copy(orow, out_hbm.at[pl.ds(0, tm)], sem_s).wait()

    def ffn_step(f, slot=None):
        if slot is None:
            slot = f % 2
        for c in weight_copies(n * n_f + f, slot):
            c.wait()
        for c in weight_copies(n * n_f + f + 1, 1 - slot):
            c.start(priority=WEIGHT_DMA_PRIORITY)
        x = xe16[...]
        gp = jnp.dot(x, wg_buf[slot].astype(jnp.bfloat16), preferred_element_type=jnp.float32)
        up = jnp.dot(x, wu_buf[slot].astype(jnp.bfloat16), preferred_element_type=jnp.float32)
        hid = (gp * jax.nn.sigmoid(gp) * up).astype(jnp.bfloat16)
        acc[...] += jnp.dot(hid, wd_buf[slot].astype(jnp.bfloat16),
                            preferred_element_type=jnp.float32)

    def run_steps(f_lo, count, row_work, per_iter=1):
        assert per_iter == 1 or (per_iter == 2 and f_lo % 2 == 0)

        def body(p, _):
            for u in range(per_iter):
                ffn_step(f_lo + per_iter * p + u, u if per_iter == 2 else None)
                row_work(per_iter * p + u)
            return 0
        lax.fori_loop(0, count // per_iter, body, 0)

    @pl.when(n == 0)
    def _():
        def body(s, _):
            xe_row(idx_cur, s).start()
            return 0
        lax.fori_loop(0, tm, body, 0, unroll=ROW_DMA_UNROLL)
        for c in weight_copies(0, 0):
            c.start(priority=WEIGHT_DMA_PRIORITY)

    wait_xe()
    packed = xe32[...]
    half = packed.shape[1]
    xe16[:, :half] = lax.bitcast_convert_type(packed << 16, jnp.float32).astype(jnp.bfloat16)
    xe16[:, half:] = lax.bitcast_convert_type(
        packed & jnp.uint32(0xFFFF0000), jnp.float32).astype(jnp.bfloat16)
    acc[...] = jnp.zeros_like(acc)

    f_lo = 0
    issued = {"scatter": 0, "next": 0, "residual": 0}
    for steps, works in _moe_schedule(tm, n_f):
        kinds = [kind for kind, _ in works]

        def row_work(k, works=works, base=dict(issued)):
            for kind, rows in works:
                for r in range(rows):
                    s = base[kind] + k * rows + r
                    if kind == "scatter":
                        scatter_row(idx_prev, s).start()
                    elif kind == "next":
                        xe_row(idx_nxt, s).start()
                    else:
                        residual_row(s).start()

        if "residual" in kinds:
            assert issued["scatter"] == tm and issued["residual"] == 0
            pl.when(n > 0)(wait_scatter)
        per_iter = 2 if sum(rows for _, rows in works) > MOE_ROWS_PER_SINGLE_STEP else 1
        if "scatter" in kinds:
            assert len(kinds) == 1
            pl.when(n > 0)(functools.partial(run_steps, f_lo, steps, row_work, per_iter))
            pl.when(n == 0)(functools.partial(run_steps, f_lo, steps, lambda k: None))
        else:
            run_steps(f_lo, steps, row_work, per_iter)
        for kind, rows in works:
            issued[kind] += steps * rows
        f_lo += steps
    assert f_lo == n_f and all(v == tm for v in issued.values())

    pltpu.make_async_copy(out_hbm.at[pl.ds(0, tm)], orow, sem_g).wait()
    g_t = gate_ref[0].T
    for j in range(tm // LANES):
        rs = slice(j * LANES, (j + 1) * LANES)
        orow[rs, :] = orow[rs, :] + acc[rs, :] * g_t[:, j:j + 1]

    @pl.when(n == n_tiles - 1)
    def _():
        def body(s, _):
            scatter_row(idx_cur, s).start()
            return 0
        lax.fori_loop(0, tm, body, 0, unroll=ROW_DMA_UNROLL)
        wait_scatter()
        wait_xe()
        for c in weight_copies(last_step, 0):
            c.wait()


def _moe(idx, gate, h2, x1, wg, wu, wd, tm, tf):
    n_exp, slots = idx.shape
    ntok, d = x1.shape
    ff = wg.shape[2]
    nt = slots // tm
    n_f = ff // tf
    n_tiles = n_exp * nt
    _moe_schedule(tm, n_f)
    idx3 = idx.reshape(n_tiles, 1, tm)
    gate3 = gate.reshape(n_exp, slots // LANES, LANES)
    smem = pltpu.MemorySpace.SMEM
    hbm = pl.BlockSpec(memory_space=pl.ANY)
    in_specs = [
        pl.BlockSpec((1, 1, tm), lambda n: (jnp.maximum(n - 1, 0), 0, 0), memory_space=smem),
        pl.BlockSpec((1, 1, tm), lambda n: (n, 0, 0), memory_space=smem),
        pl.BlockSpec((1, 1, tm), lambda n: (jnp.minimum(n + 1, n_tiles - 1), 0, 0), memory_space=smem),
        pl.BlockSpec((1, tm // LANES, LANES), lambda n: (n // nt, n % nt, 0)),
        hbm, hbm, hbm, hbm, hbm,
    ]
    return pl.pallas_call(
        functools.partial(_moe_kernel, tm=tm, tf=tf, nt=nt, n_tiles=n_tiles, n_f=n_f),
        grid=(n_tiles,), in_specs=in_specs, out_specs=hbm,
        out_shape=jax.ShapeDtypeStruct((ntok, d), jnp.float32),
        scratch_shapes=[
            pltpu.VMEM((tm, d // 2), jnp.uint32),
            pltpu.VMEM((tm, d), jnp.bfloat16),
            pltpu.VMEM((tm, d), jnp.float32),
            pltpu.VMEM((tm, d), jnp.float32),
            pltpu.VMEM((2, d, tf), jnp.float32),
            pltpu.VMEM((2, d, tf), jnp.float32),
            pltpu.VMEM((2, tf, d), jnp.float32),
            pltpu.SemaphoreType.DMA((3,)),
            pltpu.SemaphoreType.DMA((3, 2)),
        ],
        input_output_aliases={8: 0},
        compiler_params=_cparams(("arbitrary",)),
        name="expert_ffn",
    )(idx3, idx3, idx3, gate3, wg, wu, wd, h2, x1)


def _pick(n, candidates):
    for c in candidates:
        if n % c == 0:
            return c
    raise ValueError(f"no tile among {candidates} divides {n}")


def _layer(x_p, x_s, seq_p, seq_s, rel_table, norm_mix_g, w_in, q_norm_a, k_norm_a, q_norm_b,
           k_norm_b, sink_b, w_proj_a, w_proj_b, w_out, norm_ffn_g, w_router, w_gate_e, w_up_e,
           w_down_e):
    np_tok, d = x_p.shape
    ns_tok = x_s.shape[0]
    ntok = np_tok + ns_tok
    bf = jnp.bfloat16
    tm_in = _pick(math.gcd(np_tok, ns_tok), (1024, 512, 256))

    def group_cols(g):
        return [w_in[:, s * A_QKV + g * GROUP_COLS: s * A_QKV + (g + 1) * GROUP_COLS] for s in range(3)]

    w_nat = jnp.concatenate(group_cols(0) + [w_in[:, 3 * A_QKV:]], axis=1).astype(bf)
    z = _inproj(x_p, x_s, norm_mix_g, w_nat, tm_in, _pick(w_nat.shape[1], (1024, 512)))

    heads_a = tuple((h * HEAD_DIM, GROUP_COLS + h * HEAD_DIM, 2 * GROUP_COLS + h * HEAD_DIM)
                    for h in range(A_HEADS))
    heads_b = tuple((h * HEAD_DIM, B_Q + (h // B_GROUP) * HEAD_DIM,
                     B_Q + B_KV + (h // B_GROUP) * HEAD_DIM) for h in range(B_Q_HEADS))
    oas, lses = [], []
    for g in range(N_DIL_GROUPS):
        dil = DIL_RATES[g]
        radius = (DIL_WINDOWS[g] // 2) // dil
        tq = min(ATTN_TQ, seq_p // dil, seq_s // dil)
        bias = _bias_tile(rel_table[:, g * A_HEADS:(g + 1) * A_HEADS], dil, radius,
                          min(ATTN_SUB, tq))
        if dil == 1:
            zv = z
        else:
            w_g = jnp.concatenate(group_cols(g), axis=1).astype(bf)
            zv = _inproj(x_p, x_s, norm_mix_g, w_g, tm_in, w_g.shape[1], dil=dil)
        o, lse = _banded_attention(
            zv, dil=dil, col0=0, radius=radius, tq=tq, heads=heads_a,
            res_per_step=ATTN_RES_PER_STEP if (dil > 1 and tq <= ATTN_SUB) else 1, bias=bias,
            q_w=q_norm_a, k_w=k_norm_a, sink=None, with_lse=True, np_tok=np_tok, seq_p=seq_p,
            seq_s=seq_s)
        oas.append(o)
        lses.append(lse)
    tq_b = min(ATTN_TQ, seq_p, seq_s)
    bias_b = _bias_tile(rel_table[:, N_DIL_GROUPS * A_HEADS:], 1, B_RADIUS, min(ATTN_SUB_B, tq_b))
    ob = _banded_attention(
        z, dil=1, col0=QKV_COLS, radius=B_RADIUS, tq=tq_b, heads=heads_b, res_per_step=1,
        bias=bias_b, q_w=q_norm_b, k_w=k_norm_b, sink=sink_b, with_lse=False, np_tok=np_tok,
        seq_p=seq_p, seq_s=seq_s)

    x1_p, x1_s, h2_p, h2_s, aff = _merge(
        x_p, x_s, oas, lses, ob, z, w_proj_a.astype(bf), w_proj_b.astype(bf), w_out.astype(bf),
        norm_ffn_g, w_router.T.astype(bf), _pick(math.gcd(np_tok, ns_tok), (256,)))

    n_exp = w_router.shape[1]
    tf = _pick(w_gate_e.shape[2], (256, 128))
    outs = []
    for x1, h2, aff_set in ((x1_p, h2_p, aff[:np_tok // LANES]), (x1_s, h2_s, aff[np_tok // LANES:])):
        cap = max(1, EC_CAPACITY * x1.shape[0] // n_exp)
        idx, gate = _route_set(aff_set, cap, 0)
        outs.append(_moe(idx, gate, h2, x1, w_gate_e, w_up_e, w_down_e, _pick(cap, (1024,)), tf))
    return outs


def kernel(x_prompt, x_sample, rel_table, norm_mix_g, w_in, q_norm_a, k_norm_a, q_norm_b, k_norm_b,
           sink_b, w_proj_a, w_proj_b, w_out, norm_ffn_g, w_router, w_gate_e, w_up_e, w_down_e):
    bp, sp, d = x_prompt.shape
    bs, ss, _ = x_sample.shape
    x_p = x_prompt.reshape(bp * sp, d)
    x_s = x_sample.reshape(bs * ss, d)
    for l in range(norm_mix_g.shape[0]):
        x_p, x_s = _layer(x_p, x_s, sp, ss, rel_table, norm_mix_g[l], w_in[l], q_norm_a[l],
                          k_norm_a[l], q_norm_b[l], k_norm_b[l], sink_b[l], w_proj_a[l],
                          w_proj_b[l], w_out[l], norm_ffn_g[l], w_router[l], w_gate_e[l],
                          w_up_e[l], w_down_e[l])
    return x_p.reshape(bp, sp, d), x_s.reshape(bs, ss, d)
```
